```python
import jax, jax.numpy as jnp
from jax import lax
import numpy as np

D_MODEL = 2048
BATCH = 4
SEQ = 2048
DEPTH = 1

CHUNK = 64
NORM_EPS = 1e-5
N_BRANCHES = 2
RWKV_WIDTH = 1024
RWKV_HEAD = 64
RWKV_HEADS = RWKV_WIDTH // RWKV_HEAD
DECAY_LORA = 96
ICL_LORA = 96
GATE_LORA = 256
RWKV_GN_EPS = 64e-5
GLA_HEADS = 4
GLA_DK = 128
GLA_DV = 256
GLA_KEY_WIDTH = GLA_HEADS * GLA_DK
GLA_VAL_WIDTH = GLA_HEADS * GLA_DV
GLA_GATE_LORA = 16
GLA_TAU = 16.0
N_EXPERTS = 32
TOP_K = 4
D_FF = 2048
SWIGLU_LIMIT = 7.0
SWIGLU_ALPHA = 1.702
EXPERT_BLOCK = 128
RWKV_COLS = 3 * RWKV_WIDTH + DECAY_LORA + ICL_LORA + GATE_LORA
GLA_COLS = 2 * GLA_KEY_WIDTH + 2 * GLA_VAL_WIDTH + GLA_GATE_LORA
IN_COLS = RWKV_COLS + GLA_COLS + N_BRANCHES * D_MODEL

kernel_name = "hybrid_rwkv7_gla_moe_block"


def _f32(t):
    return t.astype(jnp.float32)


def _split(p, sizes):
    idx, acc = [], 0
    for s in sizes[:-1]:
        acc += s
        idx.append(acc)
    return jnp.split(p, idx, axis=-1)


def _rms_norm(x, g):
    xf = _f32(x)
    y = xf * lax.rsqrt(jnp.mean(xf * xf, axis=-1, keepdims=True) + NORM_EPS)
    return (y * _f32(g)).astype(x.dtype)


def _token_shift(p):
    return jnp.concatenate([jnp.zeros_like(p[:, :1]), p[:, :-1]], axis=1)


def _rwkv7_step(state, inp):
    r_t, w_t, k_t, v_t, kk_t, a_t = inp
    sa = jnp.einsum('bhvk,bhk->bhv', state, -kk_t)
    state = (state * w_t[:, :, None, :]
             + sa[..., None] * (kk_t * a_t)[:, :, None, :]
             + v_t[..., None] * k_t[:, :, None, :])
    y = jnp.einsum('bhvk,bhk->bhv', state, r_t)
    return state, y


def _rwkv7_branch(p, mu, w0, w2, a0, a2, g2, k_k, k_a, r_k, lnx_g, lnx_b):
    B, L, _ = p.shape
    dt = p.dtype
    p = _f32(p + (_token_shift(p) - p) * mu)
    r, k, v, wl, al, gl = _split(p, (RWKV_WIDTH, RWKV_WIDTH, RWKV_WIDTH, DECAY_LORA, ICL_LORA, GATE_LORA))
    w_log = -jax.nn.softplus(-(_f32(w0) + jnp.tanh(wl) @ _f32(w2))) - 0.5
    decay = jnp.exp(-jnp.exp(w_log))
    a = jax.nn.sigmoid(_f32(a0) + al @ _f32(a2))
    g = jax.nn.sigmoid(gl) @ _f32(g2)
    heads = lambda t: t.reshape(B, L, RWKV_HEADS, RWKV_HEAD)
    kk = heads(k * _f32(k_k))
    kk = kk / jnp.maximum(jnp.linalg.norm(kk, axis=-1, keepdims=True), 1e-12)
    k = k * (1.0 + (a - 1.0) * _f32(k_a))
    r, k, v, decay, a = heads(r), heads(k), heads(v), heads(decay), heads(a)
    xs = tuple(jnp.swapaxes(t, 0, 1) for t in (r, decay, k, v, kk, a))
    state0 = jnp.zeros((B, RWKV_HEADS, RWKV_HEAD, RWKV_HEAD), jnp.float32)
    _, y = lax.scan(_rwkv7_step, state0, xs)
    y = jnp.swapaxes(y, 0, 1)
    mean = jnp.mean(y, axis=-1, keepdims=True)
    var = jnp.mean(jnp.square(y - mean), axis=-1, keepdims=True)
    y = ((y - mean) * lax.rsqrt(var + RWKV_GN_EPS)).reshape(B, L, RWKV_WIDTH) * _f32(lnx_g) + _f32(lnx_b)
    bonus = (jnp.sum(r * k * _f32(r_k), axis=-1, keepdims=True) * v).reshape(B, L, RWKV_WIDTH)
    return ((y + bonus) * g).astype(dt)


def _gla_chunk_step(state, inp):
    q, k, v, g = inp
    b = jnp.cumsum(g, axis=2)
    diff = b[:, :, :, None, :] - b[:, :, None, :, :]
    causal = jnp.tril(jnp.ones((CHUNK, CHUNK), bool))[None, None, :, :, None]
    decay = jnp.exp(jnp.where(causal, diff, -jnp.inf))
    attn = jnp.einsum('bhtd,bhsd,bhtsd->bhts', q, k, decay)
    o = attn @ v + jnp.einsum('bhtd,bhdv->bhtv', q * jnp.exp(b), state)
    b_last = b[:, :, -1, :]
    state = (state * jnp.exp(b_last)[..., None]
             + jnp.einsum('bhsd,bhsv->bhdv', k * jnp.exp(b_last[:, :, None, :] - b), v))
    return state, o


def _gla_branch(p, w_alpha2, b_alpha, norm_g):
    B, L, _ = p.shape
    dt = p.dtype
    p = _f32(p)
    q, k, v, og, al = _split(p, (GLA_KEY_WIDTH, GLA_KEY_WIDTH, GLA_VAL_WIDTH, GLA_VAL_WIDTH, GLA_GATE_LORA))
    log_alpha = jax.nn.log_sigmoid(al @ _f32(w_alpha2) + _f32(b_alpha)) / GLA_TAU
    q = q * (GLA_DK ** -0.5)
    nc = L // CHUNK
    to_chunks = lambda t, d: t.reshape(B, nc, CHUNK, GLA_HEADS, d).transpose(1, 0, 3, 2, 4)
    xs = (to_chunks(q, GLA_DK), to_chunks(k, GLA_DK), to_chunks(v, GLA_DV), to_chunks(log_alpha, GLA_DK))
    state0 = jnp.zeros((B, GLA_HEADS, GLA_DK, GLA_DV), jnp.float32)
    _, o = lax.scan(_gla_chunk_step, state0, xs)
    o = o.transpose(1, 0, 3, 2, 4).reshape(B, L, GLA_HEADS, GLA_DV)
    o = o * lax.rsqrt(jnp.mean(o * o, axis=-1, keepdims=True) + NORM_EPS) * _f32(norm_g)
    return (o.reshape(B, L, GLA_VAL_WIDTH) * jax.nn.silu(og)).astype(dt)


def _moe(h, w_router, b_router, w_gate, b_gate, w_up, b_up, w_down, b_down):
    B, L, D = h.shape
    T = B * L
    M = T * TOP_K
    P = -(-M // EXPERT_BLOCK) * EXPERT_BLOCK + N_EXPERTS * EXPERT_BLOCK
    NB = P // EXPERT_BLOCK
    xt = h.reshape(T, D)
    logits = _f32(xt @ w_router + b_router)
    top_vals, top_idx = lax.top_k(logits, TOP_K)
    top_w = jax.nn.softmax(top_vals, axis=-1)
    expert_flat = top_idx.reshape(-1).astype(jnp.int32)
    token_flat = jnp.repeat(jnp.arange(T, dtype=jnp.int32), TOP_K)
    w_flat = top_w.reshape(-1)
    order = jnp.argsort(expert_flat, stable=True)
    sorted_e = expert_flat[order]
    counts = jnp.bincount(expert_flat, length=N_EXPERTS).astype(jnp.int32)
    padded = ((counts + EXPERT_BLOCK - 1) // EXPERT_BLOCK) * EXPERT_BLOCK
    offsets = jnp.cumsum(counts) - counts
    padded_end = jnp.cumsum(padded)
    padded_start = padded_end - padded
    rank = jnp.arange(M, dtype=jnp.int32) - offsets[sorted_e]
    dest = padded_start[sorted_e] + rank
    row_token = jnp.full((P,), T, jnp.int32).at[dest].set(token_flat[order])
    row_w = jnp.zeros((P,), jnp.float32).at[dest].set(w_flat[order])
    block_expert = jnp.minimum(
        jnp.searchsorted(padded_end, jnp.arange(NB, dtype=jnp.int32) * EXPERT_BLOCK, side='right'),
        N_EXPERTS - 1).astype(jnp.int32)
    x_pad = jnp.concatenate([xt, jnp.zeros((1, D), xt.dtype)], axis=0)
    xs = x_pad[row_token].reshape(NB, EXPERT_BLOCK, D)

    def expert_rows(args):
        xb, e = args
        gate = xb @ w_gate[e] + b_gate[e]
        up = xb @ w_up[e] + b_up[e]
        gate = jnp.minimum(gate, SWIGLU_LIMIT)
        up = jnp.clip(up, -SWIGLU_LIMIT, SWIGLU_LIMIT)
        glu = gate * jax.nn.sigmoid(gate * SWIGLU_ALPHA)
        return ((up + 1.0) * glu) @ w_down[e] + b_down[e]

    out = lax.map(expert_rows, (xs, block_expert)).reshape(P, D)
    y = jnp.zeros((T + 1, D), h.dtype).at[row_token].add(out * row_w[:, None].astype(h.dtype))[:T]
    return y.reshape(B, L, D)


def setup_inputs(seed: int = 0) -> dict:
    key = jax.random.key(seed)
    ks = iter(jax.random.split(key, 40))
    nrm = lambda shape, scale: jax.random.normal(next(ks), shape, jnp.float32) * scale
    Ly = DEPTH
    return {
        "x": nrm((BATCH, SEQ, D_MODEL), 1.0),
        "norm_mix_g": 1.0 + nrm((Ly, D_MODEL), 0.02),
        "w_in": nrm((Ly, D_MODEL, IN_COLS), D_MODEL ** -0.5),
        "token_shift_mu": jax.random.uniform(next(ks), (Ly, RWKV_COLS), jnp.float32),
        "rwkv_w0": -1.0 + nrm((Ly, RWKV_WIDTH), 0.5),
        "rwkv_w2": nrm((Ly, DECAY_LORA, RWKV_WIDTH), 0.1 * DECAY_LORA ** -0.5),
        "rwkv_a0": nrm((Ly, RWKV_WIDTH), 0.1),
        "rwkv_a2": nrm((Ly, ICL_LORA, RWKV_WIDTH), 0.1 * ICL_LORA ** -0.5),
        "rwkv_g2": nrm((Ly, GATE_LORA, RWKV_WIDTH), GATE_LORA ** -0.5),
        "rwkv_k_k": 0.85 + nrm((Ly, RWKV_WIDTH), 0.05),
        "rwkv_k_a": 1.0 + nrm((Ly, RWKV_WIDTH), 0.05),
        "rwkv_r_k": nrm((Ly, RWKV_HEADS, RWKV_HEAD), 0.1),
        "rwkv_lnx_g": 1.0 + nrm((Ly, RWKV_WIDTH), 0.02),
        "rwkv_lnx_b": nrm((Ly, RWKV_WIDTH), 0.01),
        "gla_w_alpha2": nrm((Ly, GLA_GATE_LORA, GLA_KEY_WIDTH), GLA_GATE_LORA ** -0.5),
        "gla_b_alpha": nrm((Ly, GLA_KEY_WIDTH), 0.1),
        "gla_norm_g": 1.0 + nrm((Ly, GLA_DV), 0.02),
        "b_merge": nrm((Ly, N_BRANCHES * D_MODEL), 0.01),
        "w_branch_rwkv": nrm((Ly, RWKV_WIDTH, D_MODEL), RWKV_WIDTH ** -0.5),
        "w_branch_gla": nrm((Ly, GLA_VAL_WIDTH, D_MODEL), GLA_VAL_WIDTH ** -0.5),
        "w_out": nrm((Ly, D_MODEL, D_MODEL), D_MODEL ** -0.5),
        "norm_ffn_g": 1.0 + nrm((Ly, D_MODEL), 0.02),
        "w_router": nrm((Ly, D_MODEL, N_EXPERTS), D_MODEL ** -0.5),
        "b_router": nrm((Ly, N_EXPERTS), 0.01),
        "w_gate": nrm((Ly, N_EXPERTS, D_MODEL, D_FF), D_MODEL ** -0.5),
        "b_gate": nrm((Ly, N_EXPERTS, D_FF), 0.01),
        "w_up": nrm((Ly, N_EXPERTS, D_MODEL, D_FF), D_MODEL ** -0.5),
        "b_up": nrm((Ly, N_EXPERTS, D_FF), 0.01),
        "w_down": nrm((Ly, N_EXPERTS, D_FF, D_MODEL), D_FF ** -0.5),
        "b_down": nrm((Ly, N_EXPERTS, D_MODEL), 0.01),
        "norm_final_g": 1.0 + nrm((D_MODEL,), 0.02),
    }


def reference(x, norm_mix_g, w_in, token_shift_mu, rwkv_w0, rwkv_w2, rwkv_a0, rwkv_a2, rwkv_g2,
              rwkv_k_k, rwkv_k_a, rwkv_r_k, rwkv_lnx_g, rwkv_lnx_b, gla_w_alpha2, gla_b_alpha,
              gla_norm_g, b_merge, w_branch_rwkv, w_branch_gla, w_out, norm_ffn_g, w_router,
              b_router, w_gate, b_gate, w_up, b_up, w_down, b_down, norm_final_g):
    for l in range(DEPTH):
        h = _rms_norm(x, norm_mix_g[l])
        p = h @ w_in[l]
        p_rwkv, p_gla, p_gate = _split(p, (RWKV_COLS, GLA_COLS, N_BRANCHES * D_MODEL))
        y_a = _rwkv7_branch(p_rwkv, token_shift_mu[l], rwkv_w0[l], rwkv_w2[l], rwkv_a0[l],
                            rwkv_a2[l], rwkv_g2[l], rwkv_k_k[l], rwkv_k_a[l], rwkv_r_k[l],
                            rwkv_lnx_g[l], rwkv_lnx_b[l])
        y_b = _gla_branch(p_gla, gla_w_alpha2[l], gla_b_alpha[l], gla_norm_g[l])
        gates = jax.nn.sigmoid(p_gate + b_merge[l])
        g_a, g_b = gates[..., :D_MODEL], gates[..., D_MODEL:]
        merged = g_a * (y_a @ w_branch_rwkv[l]) + g_b * (y_b @ w_branch_gla[l])
        x = x + merged @ w_out[l]
        x = x + _moe(_rms_norm(x, norm_ffn_g[l]), w_router[l], b_router[l], w_gate[l], b_gate[l],
                     w_up[l], b_up[l], w_down[l], b_down[l])
    return _rms_norm(x, norm_final_g)
```

```python
import functools

import jax
import jax.numpy as jnp
from jax import lax
from jax.experimental import pallas as pl
from jax.experimental.pallas import tpu as pltpu

F32 = jnp.float32
BF16 = jnp.bfloat16
HI = lax.Precision.HIGHEST

NORM_EPS = 1e-5
RWKV_GN_EPS = 64e-5
RWKV_WIDTH = 1024
RWKV_HEAD = 64
DECAY_LORA = 96
ICL_LORA = 96
GATE_LORA = 256
GLA_HEADS = 4
GLA_DK = 128
GLA_DV = 256
GLA_KEY_WIDTH = GLA_HEADS * GLA_DK
GLA_VAL_WIDTH = GLA_HEADS * GLA_DV
GLA_GATE_LORA = 16
GLA_TAU = 16.0
N_EXPERTS = 32
TOP_K = 4
SWIGLU_LIMIT = 7.0
SWIGLU_ALPHA = 1.702

CHUNK = 64
SUB = 16
LANE = 128

COL_GATE = 0
COL_R = 4096
COL_K = 5120
COL_V = 6144
COL_WA = 7168
COL_GL = 7424
COL_GQ = 7680
COL_GK = 8192
COL_GV = 8704
COL_GOG = 9728
COL_GAL = 10752
N_PROJ = 10880

VMEM_LIMIT = 56 * 1024 * 1024


def _mm(a, b, prec=None):
    return jnp.dot(a, b, preferred_element_type=F32, precision=prec)


def _mm_nt(a, b, prec=None):
    return lax.dot_general(a, b, (((1,), (1,)), ((), ())), preferred_element_type=F32, precision=prec)


def _mm_tn(a, b, prec=None):
    return lax.dot_general(a, b, (((0,), (0,)), ((), ())), preferred_element_type=F32, precision=prec)


def _seg_sum(x, e_ref):
    hi = x.astype(BF16)
    lo = (x - hi.astype(F32)).astype(BF16)
    e = e_ref[...]
    return _mm(hi, e) + _mm(lo, e)


def _params(sem, vmem=VMEM_LIMIT):
    return pltpu.CompilerParams(dimension_semantics=sem, vmem_limit_bytes=vmem)


def _inproj_kernel(x_ref, g_ref, w_ref, o_ref, h_ref):
    @pl.when(pl.program_id(1) == 0)
    def _():
        x = x_ref[...]
        ms = jnp.mean(x * x, axis=-1, keepdims=True)
        h_ref[...] = (x * lax.rsqrt(ms + NORM_EPS) * g_ref[...]).astype(BF16)

    o_ref[...] = _mm(h_ref[...], w_ref[...])


def _inproj(x2d, g, w_p, tm=512, tn=640):
    t, d = x2d.shape
    n = w_p.shape[1]
    return pl.pallas_call(
        _inproj_kernel,
        grid=(t // tm, n // tn),
        in_specs=[pl.BlockSpec((tm, d), lambda i, j: (i, 0)),
                  pl.BlockSpec((1, d), lambda i, j: (0, 0)),
                  pl.BlockSpec((d, tn), lambda i, j: (0, j))],
        out_specs=pl.BlockSpec((tm, tn), lambda i, j: (i, j)),
        out_shape=jax.ShapeDtypeStruct((t, n), F32),
        scratch_shapes=[pltpu.VMEM((tm, d), BF16)],
        compiler_params=_params(("parallel", "arbitrary")),
        name="inproj",
    )(x2d, g, w_p)


def _shift_mix(cur_ref, prev_ref, mu_ref, first):
    cur = cur_ref[...]
    prev_row = jnp.where(first, 0.0, prev_ref[7:8, :])
    rolled = pltpu.roll(cur, 1, axis=0)
    row = lax.broadcasted_iota(jnp.int32, cur.shape, 0)
    shifted = jnp.where(row == 0, prev_row, rolled)
    return cur + (shifted - cur) * mu_ref[...]


def _softplus(y):
    return jnp.maximum(y, 0.0) + jnp.log(1.0 + jnp.exp(-jnp.abs(y)))


def _rwkv_prep_kernel(tiles_per_seq,
                      r_ref, k_ref, v_ref, wa_ref, gl_ref,
                      rp_ref, kp_ref, vp_ref, wap_ref, glp_ref,
                      mur_ref, muk_ref, muv_ref, muwa_ref, mugl_ref,
                      w0_ref, w2_ref, a0_ref, a2_ref, g2_ref, kk_ref, ka_ref, rk_ref,
                      e_ref, tri_ref,
                      alpha_ref, beta_ref, kappa_ref, rho_ref, vout_ref, cum_ref, bonus_ref, g_ref):
    first = (pl.program_id(0) % tiles_per_seq) == 0
    r = _shift_mix(r_ref, rp_ref, mur_ref, first)
    k = _shift_mix(k_ref, kp_ref, muk_ref, first)
    v = _shift_mix(v_ref, vp_ref, muv_ref, first)
    wa = _shift_mix(wa_ref, wap_ref, muwa_ref, first)
    gl = _shift_mix(gl_ref, glp_ref, mugl_ref, first)
    wl = wa[:, :LANE]
    al = wa[:, LANE:]

    z = w0_ref[...] + _mm(jnp.tanh(wl), w2_ref[...], HI)
    w_log = -_softplus(-z) - 0.5
    logw = -jnp.exp(w_log)
    a = jax.nn.sigmoid(a0_ref[...] + _mm(al, a2_ref[...], HI))
    g_ref[...] = _mm(jax.nn.sigmoid(gl), g2_ref[...], HI)

    kk = k * kk_ref[...]
    nrm = jnp.sqrt(_seg_sum(kk * kk, e_ref))
    kk = kk / jnp.maximum(nrm, 1e-12)
    k2 = k * (1.0 + (a - 1.0) * ka_ref[...])

    cum = _mm(tri_ref[...], logw, HI)
    inv = jnp.exp(-cum)
    alpha_ref[...] = -kk * jnp.exp(cum - logw)
    beta_ref[...] = kk * a * inv
    kappa_ref[...] = k2 * inv
    rho_ref[...] = r * jnp.exp(cum)
    vout_ref[...] = v
    cum_ref[...] = cum
    bonus_ref[...] = _seg_sum(r * k2 * rk_ref[...], e_ref) * v


def _rwkv_prep(p, seq, mus, w0, w2p, a0, a2p, g2, k_k, k_a, r_k, e_seg, tm=256):
    t = p.shape[0]
    w = RWKV_WIDTH
    tiles_per_seq = seq // tm
    sub = tm // 8

    def cur(width, col):
        return pl.BlockSpec((tm, width), lambda i: (i, col // width))

    def prev(width, col):
        return pl.BlockSpec((8, width), lambda i: (jnp.maximum(i * sub - 1, 0), col // width))

    def full(shape):
        return pl.BlockSpec(shape, lambda i: (0,) * len(shape))

    rows = lax.broadcasted_iota(jnp.int32, (tm, tm), 0)
    cols = lax.broadcasted_iota(jnp.int32, (tm, tm), 1)
    tri = ((rows // CHUNK == cols // CHUNK) & (rows >= cols)).astype(F32)

    segs = [(w, COL_R), (w, COL_K), (w, COL_V), (2 * LANE, COL_WA), (GATE_LORA, COL_GL)]
    in_specs = ([cur(*s) for s in segs] + [prev(*s) for s in segs]
                + [full((1, w)), full((1, w)), full((1, w)), full((1, 2 * LANE)), full((1, GATE_LORA))]
                + [full((1, w)), full((LANE, w)), full((1, w)), full((LANE, w)), full((GATE_LORA, w)),
                   full((1, w)), full((1, w)), full((1, w)), full((w, w)), full((tm, tm))])
    out = jax.ShapeDtypeStruct((t, w), F32)
    return pl.pallas_call(
        functools.partial(_rwkv_prep_kernel, tiles_per_seq),
        grid=(t // tm,),
        in_specs=in_specs,
        out_specs=[pl.BlockSpec((tm, w), lambda i: (i, 0))] * 8,
        out_shape=[out] * 8,
        compiler_params=_params(("parallel",)),
        name="rwkv_prep",
    )(p, p, p, p, p, p, p, p, p, p, *mus, w0, w2p, a0, a2p, g2, k_k, k_a, r_k, e_seg, tri)


def _rwkv_scan_kernel(n_chunks, alpha_ref, beta_ref, kappa_ref, rho_ref, v_ref, cum_ref, y_ref, s_ref):
    @pl.when(pl.program_id(2) == 0)
    def _():
        s_ref[...] = jnp.zeros_like(s_ref)

    c2 = 2 * CHUNK
    lane = lax.broadcasted_iota(jnp.int32, (CHUNK, LANE), 1)
    head0 = lane < RWKV_HEAD
    rows = lax.broadcasted_iota(jnp.int32, (c2, c2), 0)
    cols = lax.broadcasted_iota(jnp.int32, (c2, c2), 1)
    strict = rows > cols
    incl = rows >= cols
    eye = (rows == cols).astype(F32)

    def stack(x):
        return jnp.concatenate([jnp.where(head0, x, 0.0), jnp.where(head0, 0.0, x)], axis=0)

    def body(c, carry):
        sl = pl.ds(pl.multiple_of(c * CHUNK, CHUNK), CHUNK)
        a_s = stack(alpha_ref[sl, :])
        b_s = stack(beta_ref[sl, :])
        k_s = stack(kappa_ref[sl, :])
        r_s = stack(rho_ref[sl, :])
        v_s = stack(v_ref[sl, :])
        last = pl.ds(pl.multiple_of(c * CHUNK, CHUNK) + CHUNK - 1, 1)
        g_end = jnp.exp(cum_ref[last, :])

        gram = _mm_nt(jnp.concatenate([a_s, r_s], axis=0), jnp.concatenate([b_s, k_s], axis=0), HI)
        l_ba = jnp.where(strict, gram[:c2, :c2], 0.0)
        l_ka = jnp.where(strict, gram[:c2, c2:], 0.0)
        g_b = jnp.where(incl, gram[c2:, :c2], 0.0)
        g_k = jnp.where(incl, gram[c2:, c2:], 0.0)

        t_inv = eye + l_ba
        x = l_ba
        step = 2
        while step < CHUNK:
            x = _mm(x, x, HI)
            t_inv = t_inv + _mm(t_inv, x, HI)
            step *= 2

        w2 = _mm(l_ka, v_s, HI)
        p12 = _mm(t_inv, jnp.concatenate([a_s, w2], axis=1), HI)
        p1 = p12[:, :LANE]
        p2 = p12[:, LANE:]
        qz = _mm(g_b, p12, HI)
        q = r_s + qz[:, :LANE]
        z = qz[:, LANE:] + _mm(g_k, v_s, HI)

        s = s_ref[...]
        y_st = _mm_nt(q, s, HI) + z
        y_ref[sl, :] = y_st[:CHUNK] + y_st[CHUNK:]

        m = (eye + _mm_tn(p1, b_s, HI)) * g_end
        n = (_mm_tn(p2, b_s, HI) + _mm_tn(v_s, k_s, HI)) * g_end
        s_ref[...] = _mm(s, m, HI) + n
        return carry

    lax.fori_loop(0, n_chunks, body, 0)


def _rwkv_scan(alpha, beta, kappa, rho, v, cum, batch, seq, tb=512):
    t, w = alpha.shape
    blocks_per_seq = seq // tb
    spec = pl.BlockSpec((tb, LANE), lambda b, h, i: (b * blocks_per_seq + i, h))
    return pl.pallas_call(
        functools.partial(_rwkv_scan_kernel, tb // CHUNK),
        grid=(batch, w // LANE, blocks_per_seq),
        in_specs=[spec] * 6,
        out_specs=spec,
        out_shape=jax.ShapeDtypeStruct((t, w), F32),
        scratch_shapes=[pltpu.VMEM((LANE, LANE), F32)],
        compiler_params=_params(("parallel", "parallel", "arbitrary")),
        name="rwkv_scan",
    )(alpha, beta, kappa, rho, v, cum)


def _rwkv_post_kernel(y_ref, bonus_ref, g_ref, lg_ref, lb_ref, e_ref, o_ref):
    y = y_ref[...]
    inv_n = 1.0 / RWKV_HEAD
    mean = _seg_sum(y, e_ref) * inv_n
    d = y - mean
    var = _seg_sum(d * d, e_ref) * inv_n
    yn = d * lax.rsqrt(var + RWKV_GN_EPS) * lg_ref[...] + lb_ref[...]
    o_ref[...] = ((yn + bonus_ref[...]) * g_ref[...]).astype(o_ref.dtype)


def _rwkv_post(y, bonus, g, lnx_g, lnx_b, e_seg, tm=512):
    t, w = y.shape
    row = pl.BlockSpec((tm, w), lambda i: (i, 0))
    vec = pl.BlockSpec((1, w), lambda i: (0, 0))
    return pl.pallas_call(
        _rwkv_post_kernel,
        grid=(t // tm,),
        in_specs=[row, row, row, vec, vec, pl.BlockSpec((w, w), lambda i: (0, 0))],
        out_specs=row,
        out_shape=jax.ShapeDtypeStruct((t, w), BF16),
        compiler_params=_params(("parallel",)),
        name="rwkv_post",
    )(y, bonus, g, lnx_g, lnx_b, e_seg)


def _gla_kernel(n_chunks, q_ref, k_ref, v_ref, og_ref, al_ref, wal_ref, bal_ref, ng_ref, o_ref, s_ref):
    @pl.when(pl.program_id(2) == 0)
    def _():
        s_ref[...] = jnp.zeros_like(s_ref)

    n_sub = CHUNK // SUB
    scale = GLA_DK ** -0.5
    rows = lax.broadcasted_iota(jnp.int32, (CHUNK, CHUNK), 0)
    cols = lax.broadcasted_iota(jnp.int32, (CHUNK, CHUNK), 1)
    tri = (rows >= cols).astype(F32)
    srow = lax.broadcasted_iota(jnp.int32, (CHUNK, LANE), 0)
    slane = lax.broadcasted_iota(jnp.int32, (CHUNK, LANE), 1)

    def body(c, carry):
        sl = pl.ds(pl.multiple_of(c * CHUNK, CHUNK), CHUNK)
        q = q_ref[sl, :] * scale
        k = k_ref[sl, :]
        v = v_ref[sl, :]
        pre = _mm(al_ref[sl, :], wal_ref[...], HI) + bal_ref[...]
        g = jax.nn.log_sigmoid(pre) / GLA_TAU
        b = _mm(tri, g, HI)

        pieces = []
        for i in range(n_sub):
            lo, hi = i * SUB, (i + 1) * SUB
            base = b[lo - 1:lo, :] if i > 0 else jnp.zeros((1, LANE), F32)
            q_i = q[lo:hi, :] * jnp.exp(b[lo:hi, :] - base)
            if i > 0:
                k_hat = k[:lo, :] * jnp.exp(base - b[:lo, :])
                top = _mm_nt(k_hat, q_i, HI)
                pieces.append(jnp.concatenate([top, jnp.zeros((CHUNK - lo, SUB), F32)], axis=0))
            else:
                pieces.append(jnp.zeros((CHUNK, SUB), F32))
        pieces.append(jnp.zeros((CHUNK, LANE - CHUNK), F32))
        a_t = jnp.concatenate(pieces, axis=1)

        for tau in range(SUB):
            q_b = jnp.concatenate([jnp.broadcast_to(q[i * SUB + tau:i * SUB + tau + 1, :], (SUB, LANE))
                                   for i in range(n_sub)], axis=0)
            b_b = jnp.concatenate([jnp.broadcast_to(b[i * SUB + tau:i * SUB + tau + 1, :], (SUB, LANE))
                                   for i in range(n_sub)], axis=0)
            ok = (srow % SUB) <= tau
            dec = jnp.where(ok, jnp.exp(jnp.where(ok, b_b - b, 0.0)), 0.0)
            col = jnp.sum(k * q_b * dec, axis=-1, keepdims=True)
            hit = (slane % SUB == tau) & (slane // SUB == srow // SUB)
            a_t = a_t + jnp.where(hit, col, 0.0)

        s = s_ref[...]
        o = _mm_tn(a_t[:, :CHUNK], v, HI) + _mm_nt(q * jnp.exp(b), s, HI)
        b_last = b[CHUNK - 1:CHUNK, :]
        s_ref[...] = s * jnp.exp(b_last) + _mm_tn(v, k * jnp.exp(b_last - b), HI)

        o = o * lax.rsqrt(jnp.mean(o * o, axis=-1, keepdims=True) + NORM_EPS) * ng_ref[...]
        og = og_ref[sl, :]
        o_ref[sl, :] = (o * (og * jax.nn.sigmoid(og))).astype(o_ref.dtype)
        return carry

    lax.fori_loop(0, n_chunks, body, 0)


def _gla(p, w_alpha2p, b_alpha, norm_g, batch, seq, tb=256):
    t = p.shape[0]
    blocks_per_seq = seq // tb

    def col(width, base):
        return pl.BlockSpec((tb, width), lambda b, h, i: (b * blocks_per_seq + i, base // width + h))

    return pl.pallas_call(
        functools.partial(_gla_kernel, tb // CHUNK),
        grid=(batch, GLA_HEADS, blocks_per_seq),
        in_specs=[col(GLA_DK, COL_GQ), col(GLA_DK, COL_GK), col(GLA_DV, COL_GV), col(GLA_DV, COL_GOG),
                  pl.BlockSpec((tb, LANE), lambda b, h, i: (b * blocks_per_seq + i, COL_GAL // LANE)),
                  pl.BlockSpec((LANE, GLA_DK), lambda b, h, i: (0, h)),
                  pl.BlockSpec((1, GLA_DK), lambda b, h, i: (0, h)),
                  pl.BlockSpec((1, GLA_DV), lambda b, h, i: (0, 0))],
        out_specs=pl.BlockSpec((tb, GLA_DV), lambda b, h, i: (b * blocks_per_seq + i, h)),
        out_shape=jax.ShapeDtypeStruct((t, GLA_VAL_WIDTH), BF16),
        scratch_shapes=[pltpu.VMEM((GLA_DV, GLA_DK), F32)],
        compiler_params=_params(("parallel", "parallel", "arbitrary")),
        name="gla",
    )(p, p, p, p, p, w_alpha2p, b_alpha, norm_g)


def _merge_kernel(ya_ref, yb_ref, pg_ref, bm_ref, wa_ref, wb_ref, o_ref):
    d = wa_ref.shape[1]
    gates = jax.nn.sigmoid(pg_ref[...] + bm_ref[...])
    ma = _mm(ya_ref[...], wa_ref[...])
    mb = _mm(yb_ref[...], wb_ref[...])
    o_ref[...] = (gates[:, :d] * ma + gates[:, d:] * mb).astype(o_ref.dtype)


def _merge(ya, yb, p, b_merge, wa, wb, tm=256):
    t = ya.shape[0]
    d = wa.shape[1]
    return pl.pallas_call(
        _merge_kernel,
        grid=(t // tm,),
        in_specs=[pl.BlockSpec((tm, ya.shape[1]), lambda i: (i, 0)),
                  pl.BlockSpec((tm, yb.shape[1]), lambda i: (i, 0)),
                  pl.BlockSpec((tm, 2 * d), lambda i: (i, COL_GATE // (2 * d))),
                  pl.BlockSpec((1, 2 * d), lambda i: (0, 0)),
                  pl.BlockSpec(wa.shape, lambda i: (0, 0)),
                  pl.BlockSpec(wb.shape, lambda i: (0, 0))],
        out_specs=pl.BlockSpec((tm, d), lambda i: (i, 0)),
        out_shape=jax.ShapeDtypeStruct((t, d), BF16),
        compiler_params=_params(("parallel",)),
        name="merge",
    )(ya, yb, p, b_merge, wa, wb)


def _outproj_kernel(m_ref, x_ref, wo_ref, g_ref, wr_ref, br_ref, x1_ref, h_ref, lg_ref):
    x1 = x_ref[...] + _mm(m_ref[...], wo_ref[...])
    x1_ref[...] = x1
    ms = jnp.mean(x1 * x1, axis=-1, keepdims=True)
    h = x1 * lax.rsqrt(ms + NORM_EPS) * g_ref[...]
    h_ref[...] = h
    lg_ref[...] = _mm(h, wr_ref[...], HI) + br_ref[...]


def _outproj(merged, x2d, w_out, g_ffn, w_router_p, b_router_p, tm=256):
    t, d = x2d.shape
    row = pl.BlockSpec((tm, d), lambda i: (i, 0))
    return pl.pallas_call(
        _outproj_kernel,
        grid=(t // tm,),
        in_specs=[row, row,
                  pl.BlockSpec((d, d), lambda i: (0, 0)),
                  pl.BlockSpec((1, d), lambda i: (0, 0)),
                  pl.BlockSpec((d, LANE), lambda i: (0, 0)),
                  pl.BlockSpec((1, LANE), lambda i: (0, 0))],
        out_specs=[row, row, pl.BlockSpec((tm, LANE), lambda i: (i, 0))],
        out_shape=[jax.ShapeDtypeStruct((t, d), F32), jax.ShapeDtypeStruct((t, d), F32),
                   jax.ShapeDtypeStruct((t, LANE), F32)],
        compiler_params=_params(("parallel",)),
        name="outproj",
    )(merged, x2d, w_out, g_ffn, w_router_p, b_router_p)


def _row_copy(src_ref, src_row, dst_ref, dst_row, sem):
    return pltpu.make_async_copy(src_ref.at[pl.ds(src_row, 1)], dst_ref.at[pl.ds(dst_row, 1)], sem)


def _dispatch_kernel(tm, dest_ref, h_ref, xs_in_ref, xs_ref, sem):
    del xs_in_ref

    def issue(r, carry):
        for j in range(TOP_K):
            _row_copy(h_ref, r, xs_ref, dest_ref[0, 0, r * TOP_K + j], sem).start()
        return carry

    lax.fori_loop(0, tm, issue, 0)

    def drain(r, carry):
        for j in range(TOP_K):
            _row_copy(h_ref, r, xs_ref, dest_ref[0, 0, r * TOP_K + j], sem).wait()
        return carry

    lax.fori_loop(0, tm, drain, 0)


def _dispatch(h, dest, n_rows, tm=256):
    t, d = h.shape
    xs0 = jnp.zeros((n_rows, d), h.dtype)
    dest3 = dest.reshape(t // tm, 1, tm * TOP_K)
    return pl.pallas_call(
        functools.partial(_dispatch_kernel, tm),
        grid=(t // tm,),
        in_specs=[pl.BlockSpec((1, 1, tm * TOP_K), lambda i: (i, 0, 0), memory_space=pltpu.SMEM),
                  pl.BlockSpec((tm, d), lambda i: (i, 0)),
                  pl.BlockSpec(memory_space=pl.ANY)],
        out_specs=pl.BlockSpec(memory_space=pl.ANY),
        out_shape=jax.ShapeDtypeStruct((n_rows, d), h.dtype),
        scratch_shapes=[pltpu.SemaphoreType.DMA(())],
        input_output_aliases={2: 0},
        compiler_params=_params(("arbitrary",)),
        name="dispatch",
    )(dest3, h, xs0)


def _expert_kernel(te_ref, tv_ref, nv_ref, x_ref, wg_ref, bg_ref, wu_ref, bu_ref, wd_ref, bd_ref, o_ref, xb_ref):
    w = pl.program_id(0)
    f = pl.program_id(1)

    @pl.when((tv_ref[w] == 0) & (f == 0))
    def _():
        o_ref[...] = jnp.zeros_like(o_ref)

    @pl.when(tv_ref[w] == 1)
    def _():
        @pl.when(f == 0)
        def _():
            xb_ref[...] = x_ref[...].astype(BF16)
            o_ref[...] = jnp.broadcast_to(bd_ref[0], o_ref.shape)

        xb = xb_ref[...]
        gate = _mm(xb, wg_ref[0].astype(BF16)) + bg_ref[0]
        up = _mm(xb, wu_ref[0].astype(BF16)) + bu_ref[0]
        gate = jnp.minimum(gate, SWIGLU_LIMIT)
        up = jnp.clip(up, -SWIGLU_LIMIT, SWIGLU_LIMIT)
        glu = gate * jax.nn.sigmoid(gate * SWIGLU_ALPHA)
        act = ((up + 1.0) * glu).astype(BF16)
        o_ref[...] += _mm(act, wd_ref[0].astype(BF16))


def _experts(xs, tile_expert, tile_valid, n_valid, w_gate, b_gate, w_up, b_up, w_down, b_down, tm, tf=512):
    n_rows, d = xs.shape
    n_e, _, d_ff = w_gate.shape
    n_tiles = n_rows // tm
    n_f = d_ff // tf

    def tile(w, te, tv, nv):
        return jnp.minimum(w, nv[0] - 1)

    def fidx(w, f, tv):
        return jnp.where(tv[w] == 1, f, n_f - 1)

    grid_spec = pltpu.PrefetchScalarGridSpec(
        num_scalar_prefetch=3,
        grid=(n_tiles, n_f),
        in_specs=[pl.BlockSpec((tm, d), lambda w, f, te, tv, nv: (tile(w, te, tv, nv), 0)),
                  pl.BlockSpec((1, d, tf), lambda w, f, te, tv, nv: (te[w], 0, fidx(w, f, tv))),
                  pl.BlockSpec((1, 1, tf), lambda w, f, te, tv, nv: (te[w], 0, fidx(w, f, tv))),
                  pl.BlockSpec((1, d, tf), lambda w, f, te, tv, nv: (te[w], 0, fidx(w, f, tv))),
                  pl.BlockSpec((1, 1, tf), lambda w, f, te, tv, nv: (te[w], 0, fidx(w, f, tv))),
                  pl.BlockSpec((1, tf, d), lambda w, f, te, tv, nv: (te[w], fidx(w, f, tv), 0)),
                  pl.BlockSpec((1, 1, d), lambda w, f, te, tv, nv: (te[w], 0, 0))],
        out_specs=pl.BlockSpec((tm, d), lambda w, f, te, tv, nv: (w, 0)),
        scratch_shapes=[pltpu.VMEM((tm, d), BF16)],
    )
    return pl.pallas_call(
        _expert_kernel,
        grid_spec=grid_spec,
        out_shape=jax.ShapeDtypeStruct((n_rows, d), F32),
        compiler_params=_params(("arbitrary", "arbitrary")),
        name="experts",
    )(tile_expert, tile_valid, n_valid, xs, w_gate, b_gate.reshape(n_e, 1, d_ff), w_up,
      b_up.reshape(n_e, 1, d_ff), w_down, b_down.reshape(n_e, 1, d))


def _combine_kernel(tm, final, pos_ref, eo_ref, x1_ref, wt_ref, g_ref, o_ref, buf_ref, sem):
    def issue(r, carry):
        for j in range(TOP_K):
            _row_copy(eo_ref, pos_ref[0, 0, r * TOP_K + j], buf_ref.at[j], r, sem).start()
        return carry

    lax.fori_loop(0, tm, issue, 0)

    def drain(r, carry):
        for j in range(TOP_K):
            _row_copy(eo_ref, pos_ref[0, 0, r * TOP_K + j], buf_ref.at[j], r, sem).wait()
        return carry

    lax.fori_loop(0, tm, drain, 0)

    wt = wt_ref[...]
    y = x1_ref[...]
    for j in range(TOP_K):
        y = y + buf_ref[j] * wt[:, j:j + 1]
    if final:
        y = y * lax.rsqrt(jnp.mean(y * y, axis=-1, keepdims=True) + NORM_EPS) * g_ref[...]
    o_ref[...] = y


def _combine(eo, pos, x1, top_w, g_final, final, tm=128):
    t, d = x1.shape
    pos3 = pos.reshape(t // tm, 1, tm * TOP_K)
    return pl.pallas_call(
        functools.partial(_combine_kernel, tm, final),
        grid=(t // tm,),
        in_specs=[pl.BlockSpec((1, 1, tm * TOP_K), lambda i: (i, 0, 0), memory_space=pltpu.SMEM),
                  pl.BlockSpec(memory_space=pl.ANY),
                  pl.BlockSpec((tm, d), lambda i: (i, 0)),
                  pl.BlockSpec((tm, TOP_K), lambda i: (i, 0)),
                  pl.BlockSpec((1, d), lambda i: (0, 0))],
        out_specs=pl.BlockSpec((tm, d), lambda i: (i, 0)),
        out_shape=jax.ShapeDtypeStruct((t, d), F32),
        scratch_shapes=[pltpu.VMEM((TOP_K, tm, d), F32), pltpu.SemaphoreType.DMA(())],
        compiler_params=_params(("arbitrary",)),
        name="combine",
    )(pos3, eo, x1, top_w, g_final)


def _routing(logits, tm):
    t = logits.shape[0]
    top_vals, top_idx = lax.top_k(logits[:, :N_EXPERTS], TOP_K)
    top_w = jax.nn.softmax(top_vals, axis=-1)
    e_flat = top_idx.reshape(-1).astype(jnp.int32)
    onehot = (e_flat[:, None] == jnp.arange(N_EXPERTS, dtype=jnp.int32)[None, :]).astype(jnp.int32)
    incl = jnp.cumsum(onehot, axis=0)
    counts = incl[-1]
    rank = jnp.sum((incl - onehot) * onehot, axis=1)
    tiles = (counts + tm - 1) // tm
    tile_end = jnp.cumsum(tiles)
    tile_start = tile_end - tiles
    dest = jnp.sum(onehot * (tile_start * tm)[None, :], axis=1) + rank
    n_tiles = (t * TOP_K) // tm + N_EXPERTS
    wid = jnp.arange(n_tiles, dtype=jnp.int32)
    n_valid = tile_end[-1:].astype(jnp.int32)
    tile_expert = jnp.minimum(jnp.searchsorted(tile_end, jnp.minimum(wid, n_valid[0] - 1), side="right"),
                              N_EXPERTS - 1).astype(jnp.int32)
    tile_valid = (wid < n_valid[0]).astype(jnp.int32)
    return top_w, dest.astype(jnp.int32), tile_expert, tile_valid, n_valid, n_tiles * tm


def _pad_cols(a, width):
    return jnp.pad(a, ((0, 0), (0, width - a.shape[1])))


def _pad_rows(a, height):
    return jnp.pad(a, ((0, height - a.shape[0]), (0, 0)))


def _split_cols(a, sizes):
    out, acc = [], 0
    for s in sizes:
        out.append(a[..., acc:acc + s])
        acc += s
    return out


def kernel(x, norm_mix_g, w_in, token_shift_mu, rwkv_w0, rwkv_w2, rwkv_a0, rwkv_a2, rwkv_g2, rwkv_k_k, rwkv_k_a, rwkv_r_k, rwkv_lnx_g, rwkv_lnx_b, gla_w_alpha2, gla_b_alpha, gla_norm_g, b_merge, w_branch_rwkv, w_branch_gla, w_out, norm_ffn_g, w_router, b_router, w_gate, b_gate, w_up, b_up, w_down, b_down, norm_final_g):
    batch, seq, d = x.shape
    t = batch * seq
    depth = w_in.shape[0]
    moe_tm = 512
    w = RWKV_WIDTH
    sizes = (w, w, w, DECAY_LORA, ICL_LORA, GATE_LORA, GLA_KEY_WIDTH, GLA_KEY_WIDTH, GLA_VAL_WIDTH,
             GLA_VAL_WIDTH, GLA_GATE_LORA, 2 * d)

    rows = lax.broadcasted_iota(jnp.int32, (w, w), 0)
    cols = lax.broadcasted_iota(jnp.int32, (w, w), 1)
    e_seg = (rows // RWKV_HEAD == cols // RWKV_HEAD).astype(BF16)

    x2d = x.reshape(t, d)
    for l in range(depth):
        s_r, s_k, s_v, s_wl, s_al, s_gl, s_gq, s_gk, s_gv, s_gog, s_gal, s_gate = _split_cols(w_in[l], sizes)
        w_p = jnp.concatenate(
            [s_gate, s_r, s_k, s_v, _pad_cols(s_wl, LANE), _pad_cols(s_al, LANE), s_gl, s_gq, s_gk, s_gv,
             s_gog, _pad_cols(s_gal, LANE)], axis=1).astype(BF16)
        mu = token_shift_mu[l][None, :]
        m_r, m_k, m_v, m_wl, m_al, m_gl = _split_cols(mu, sizes[:6])
        mus = (m_r, m_k, m_v, jnp.concatenate([_pad_cols(m_wl, LANE), _pad_cols(m_al, LANE)], axis=1), m_gl)

        p = _inproj(x2d, norm_mix_g[l][None, :], w_p)

        alpha, beta, kappa, rho, vv, cum, bonus, g = _rwkv_prep(
            p, seq, mus, rwkv_w0[l][None, :], _pad_rows(rwkv_w2[l], LANE), rwkv_a0[l][None, :],
            _pad_rows(rwkv_a2[l], LANE), rwkv_g2[l], rwkv_k_k[l][None, :], rwkv_k_a[l][None, :],
            rwkv_r_k[l].reshape(1, w), e_seg)
        y = _rwkv_scan(alpha, beta, kappa, rho, vv, cum, batch, seq)
        ya = _rwkv_post(y, bonus, g, rwkv_lnx_g[l][None, :], rwkv_lnx_b[l][None, :], e_seg)

        yb = _gla(p, _pad_rows(gla_w_alpha2[l], LANE), gla_b_alpha[l][None, :], gla_norm_g[l][None, :],
                  batch, seq)

        merged = _merge(ya, yb, p, b_merge[l][None, :], w_branch_rwkv[l].astype(BF16),
                        w_branch_gla[l].astype(BF16))
        b_router_p = jnp.concatenate([b_router[l], jnp.full((LANE - N_EXPERTS,), -1e30, F32)])[None, :]
        x1, h2, logits = _outproj(merged, x2d, w_out[l].astype(BF16), norm_ffn_g[l][None, :],
                                  _pad_cols(w_router[l], LANE), b_router_p)

        top_w, dest, tile_expert, tile_valid, n_valid, n_rows = _routing(logits, moe_tm)
        xs = _dispatch(h2, dest, n_rows)
        eo = _experts(xs, tile_expert, tile_valid, n_valid, w_gate[l], b_gate[l], w_up[l], b_up[l],
                      w_down[l], b_down[l], moe_tm)
        final = l == depth - 1
        x2d = _combine(eo, dest, x1, top_w, norm_final_g[None, :], final)
    return x2d.reshape(batch, seq, d)
```

```python
import functools

import jax
import jax.numpy as jnp
from jax import lax
from jax.experimental import pallas as pl
from jax.experimental.pallas import tpu as pltpu

F32 = jnp.float32
BF16 = jnp.bfloat16
HI = lax.Precision.HIGHEST

NORM_EPS = 1e-5
RWKV_GN_EPS = 64e-5
RWKV_WIDTH = 1024
RWKV_HEAD = 64
DECAY_LORA = 96
ICL_LORA = 96
GATE_LORA = 256
GLA_HEADS = 4
GLA_DK = 128
GLA_DV = 256
GLA_KEY_WIDTH = GLA_HEADS * GLA_DK
GLA_VAL_WIDTH = GLA_HEADS * GLA_DV
GLA_GATE_LORA = 16
GLA_TAU = 16.0
N_EXPERTS = 32
TOP_K = 4
SWIGLU_LIMIT = 7.0
SWIGLU_ALPHA = 1.702

CHUNK = 64
SUB = 16
LANE = 128

COL_GATE = 0
COL_R = 4096
COL_K = 5120
COL_V = 6144
COL_WA = 7168
COL_GL = 7424
COL_GQ = 7680
COL_GK = 8192
COL_GV = 8704
COL_GOG = 9728
COL_GAL = 10752
N_PROJ = 10880

VMEM_LIMIT = 56 * 1024 * 1024

_DIMS = {"nn": (((1,), (0,)), ((), ())), "nt": (((1,), (1,)), ((), ())), "tn": (((0,), (0,)), ((), ()))}


def _mm(a, b, prec=None):
    return jnp.dot(a, b, preferred_element_type=F32, precision=prec)


def _bdot(kind, a, b):
    return lax.dot_general(a.astype(BF16), b.astype(BF16), _DIMS[kind], preferred_element_type=F32)


def _split_bf16(a):
    hi = a.astype(BF16)
    return hi, (a - hi.astype(F32)).astype(BF16)


def _dot3(kind, a, b):
    ah, al = _split_bf16(a)
    bh, bl = _split_bf16(b)
    return _bdot(kind, ah, bh) + _bdot(kind, ah, bl) + _bdot(kind, al, bh)


def _seg_sum(x, e_ref):
    hi, lo = _split_bf16(x)
    e = e_ref[...]
    return _mm(hi, e) + _mm(lo, e)


def _params(sem, vmem=VMEM_LIMIT):
    return pltpu.CompilerParams(dimension_semantics=sem, vmem_limit_bytes=vmem)


def _inproj_kernel(x_ref, g_ref, w_ref, o_ref, h_ref):
    @pl.when(pl.program_id(1) == 0)
    def _():
        x = x_ref[...]
        ms = jnp.mean(x * x, axis=-1, keepdims=True)
        h_ref[...] = (x * lax.rsqrt(ms + NORM_EPS) * g_ref[...]).astype(BF16)

    o_ref[...] = _mm(h_ref[...], w_ref[...])


def _inproj(x2d, g, w_p, tm=512, tn=640):
    t, d = x2d.shape
    n = w_p.shape[1]
    return pl.pallas_call(
        _inproj_kernel,
        grid=(t // tm, n // tn),
        in_specs=[pl.BlockSpec((tm, d), lambda i, j: (i, 0)),
                  pl.BlockSpec((1, d), lambda i, j: (0, 0)),
                  pl.BlockSpec((d, tn), lambda i, j: (0, j))],
        out_specs=pl.BlockSpec((tm, tn), lambda i, j: (i, j)),
        out_shape=jax.ShapeDtypeStruct((t, n), F32),
        scratch_shapes=[pltpu.VMEM((tm, d), BF16)],
        compiler_params=_params(("parallel", "arbitrary")),
        name="inproj",
    )(x2d, g, w_p)


def _softplus(y):
    return jnp.maximum(y, 0.0) + jnp.log(1.0 + jnp.exp(-jnp.abs(y)))


def _rwkv_kernel(tb, r_ref, k_ref, v_ref, wa_ref, gl_ref,
                 mur_ref, muk_ref, muv_ref, muwa_ref, mugl_ref,
                 w0_ref, w2_ref, a0_ref, a2_ref, g2_ref, kk_ref, ka_ref, rk_ref, lg_ref, lb_ref,
                 tri_ref, e_ref, o_ref,
                 s_ref, pr_ref, pk_ref, pv_ref, pwa_ref, pgl_ref,
                 ah_ref, al_ref, bh_ref, bl_ref, kh_ref, kl_ref, rh_ref, rl_ref, vb_ref,
                 cum_ref, y_ref, bonus_ref, g_ref):
    @pl.when(pl.program_id(2) == 0)
    def _():
        s_ref[...] = jnp.zeros_like(s_ref)
        for ref in (pr_ref, pk_ref, pv_ref, pwa_ref, pgl_ref):
            ref[...] = jnp.zeros_like(ref)

    def shift_mix(cur_ref, prev_ref, mu_ref):
        cur = cur_ref[...]
        row = lax.broadcasted_iota(jnp.int32, cur.shape, 0)
        shifted = jnp.where(row == 0, prev_ref[0:1, :], pltpu.roll(cur, 1, axis=0))
        prev_ref[0:1, :] = cur[tb - 1:tb, :]
        return cur + (shifted - cur) * mu_ref[...]

    r = shift_mix(r_ref, pr_ref, mur_ref)
    k = shift_mix(k_ref, pk_ref, muk_ref)
    v = shift_mix(v_ref, pv_ref, muv_ref)
    wa = shift_mix(wa_ref, pwa_ref, muwa_ref)
    gl = shift_mix(gl_ref, pgl_ref, mugl_ref)

    z = w0_ref[...] + _dot3("nn", jnp.tanh(wa[:, :LANE]), w2_ref[...])
    logw = -jnp.exp(-_softplus(-z) - 0.5)
    a = jax.nn.sigmoid(a0_ref[...] + _dot3("nn", wa[:, LANE:], a2_ref[...]))
    g_ref[...] = _bdot("nn", jax.nn.sigmoid(gl), g2_ref[...])

    kk = k * kk_ref[...]
    kk = kk / jnp.maximum(jnp.sqrt(_seg_sum(kk * kk, e_ref)), 1e-12)
    k2 = k * (1.0 + (a - 1.0) * ka_ref[...])
    bonus_ref[...] = _seg_sum(r * k2 * rk_ref[...], e_ref) * v

    lw_hi, lw_lo = _split_bf16(logw)
    cum = _mm(tri_ref[...], lw_hi) + _mm(tri_ref[...], lw_lo)
    inv = jnp.exp(-cum)
    cum_ref[...] = cum
    for val, hi_ref, lo_ref in ((-kk * jnp.exp(cum - logw), ah_ref, al_ref), (kk * a * inv, bh_ref, bl_ref),
                                (k2 * inv, kh_ref, kl_ref), (r * jnp.exp(cum), rh_ref, rl_ref)):
        hi, lo = _split_bf16(val)
        hi_ref[...] = hi
        lo_ref[...] = lo
    vb_ref[...] = v.astype(BF16)

    c2 = 2 * CHUNK
    lane = lax.broadcasted_iota(jnp.int32, (CHUNK, LANE), 1)
    head0 = lane < RWKV_HEAD
    rows = lax.broadcasted_iota(jnp.int32, (c2, c2), 0)
    cols = lax.broadcasted_iota(jnp.int32, (c2, c2), 1)
    strict = rows > cols
    incl = rows >= cols
    eye = (rows == cols).astype(F32)
    zero = jnp.zeros((CHUNK, LANE), BF16)

    def stack(ref, sl):
        x = ref[sl, :]
        return jnp.concatenate([jnp.where(head0, x, zero), jnp.where(head0, zero, x)], axis=0)

    def body(c, carry):
        start = pl.multiple_of(c * CHUNK, CHUNK)
        sl = pl.ds(start, CHUNK)
        a_h, b_h, k_h, r_h, v_s = (stack(ref, sl) for ref in (ah_ref, bh_ref, kh_ref, rh_ref, vb_ref))
        a_l, b_l, k_l, r_l = (stack(ref, sl) for ref in (al_ref, bl_ref, kl_ref, rl_ref))
        g_end = jnp.exp(cum_ref[pl.ds(start + CHUNK - 1, 1), :])

        ar_h = jnp.concatenate([a_h, r_h], axis=0)
        bk_h = jnp.concatenate([b_h, k_h], axis=0)
        gram = (_bdot("nt", ar_h, bk_h) + _bdot("nt", ar_h, jnp.concatenate([b_l, k_l], axis=0))
                + _bdot("nt", jnp.concatenate([a_l, r_l], axis=0), bk_h))
        l_ba = jnp.where(strict, gram[:c2, :c2], 0.0)
        l_ka = jnp.where(strict, gram[:c2, c2:], 0.0)
        g_b = jnp.where(incl, gram[c2:, :c2], 0.0)
        g_k = jnp.where(incl, gram[c2:, c2:], 0.0)

        t_inv = eye + l_ba
        x = l_ba
        step = 2
        while step < CHUNK:
            x = _bdot("nn", x, x)
            t_inv = t_inv + _bdot("nn", t_inv, x)
            step *= 2

        w2 = _bdot("nn", l_ka, v_s)
        p12 = _bdot("nn", t_inv, jnp.concatenate([a_h, w2.astype(BF16)], axis=1)).astype(BF16)
        qz = _bdot("nn", g_b, p12)
        q = r_h.astype(F32) + r_l.astype(F32) + qz[:, :LANE]
        z_c = qz[:, LANE:] + _bdot("nn", g_k, v_s)

        s = s_ref[...]
        y_st = _bdot("nt", q, s) + z_c
        y_ref[sl, :] = y_st[:CHUNK] + y_st[CHUNK:]

        mn = _bdot("tn", p12, b_h)
        m = (eye + mn[:LANE]) * g_end
        n = (mn[LANE:] + _bdot("tn", v_s, k_h)) * g_end
        s_ref[...] = _bdot("nn", s, m) + n
        return carry

    lax.fori_loop(0, tb // CHUNK, body, 0)

    y = y_ref[...]
    inv_n = 1.0 / RWKV_HEAD
    mean = _seg_sum(y, e_ref) * inv_n
    d = y - mean
    var = _seg_sum(d * d, e_ref) * inv_n
    yn = d * lax.rsqrt(var + RWKV_GN_EPS) * lg_ref[...] + lb_ref[...]
    o_ref[...] = ((yn + bonus_ref[...]) * g_ref[...]).astype(o_ref.dtype)


def _rwkv(p, batch, seq, mus, w0, w2p, a0, a2p, g2, k_k, k_a, r_k, lnx_g, lnx_b, tb=512):
    t = p.shape[0]
    w = RWKV_WIDTH
    blocks_per_seq = seq // tb

    def seg(width, base, per_head):
        return pl.BlockSpec((tb, width),
                            lambda b, h, i: (b * blocks_per_seq + i, base // width + (h if per_head else 0)))

    def vec(width, per_head=True):
        return pl.BlockSpec((1, width), lambda b, h, i: (0, h if per_head else 0))

    def mat(rows_):
        return pl.BlockSpec((rows_, LANE), lambda b, h, i: (0, h))

    rows = lax.broadcasted_iota(jnp.int32, (tb, tb), 0)
    cols = lax.broadcasted_iota(jnp.int32, (tb, tb), 1)
    tri = ((rows // CHUNK == cols // CHUNK) & (rows >= cols)).astype(BF16)
    rows = lax.broadcasted_iota(jnp.int32, (LANE, LANE), 0)
    cols = lax.broadcasted_iota(jnp.int32, (LANE, LANE), 1)
    e_seg = (rows // RWKV_HEAD == cols // RWKV_HEAD).astype(BF16)

    in_specs = [seg(LANE, COL_R, True), seg(LANE, COL_K, True), seg(LANE, COL_V, True),
                seg(2 * LANE, COL_WA, False), seg(GATE_LORA, COL_GL, False),
                vec(LANE), vec(LANE), vec(LANE), vec(2 * LANE, False), vec(GATE_LORA, False),
                vec(LANE), mat(LANE), vec(LANE), mat(LANE), mat(GATE_LORA), vec(LANE), vec(LANE), vec(LANE),
                vec(LANE), vec(LANE),
                pl.BlockSpec((tb, tb), lambda b, h, i: (0, 0)),
                pl.BlockSpec((LANE, LANE), lambda b, h, i: (0, 0))]
    half = pltpu.VMEM((tb, LANE), BF16)
    full = pltpu.VMEM((tb, LANE), F32)
    return pl.pallas_call(
        functools.partial(_rwkv_kernel, tb),
        grid=(batch, w // LANE, blocks_per_seq),
        in_specs=in_specs,
        out_specs=pl.BlockSpec((tb, LANE), lambda b, h, i: (b * blocks_per_seq + i, h)),
        out_shape=jax.ShapeDtypeStruct((t, w), BF16),
        scratch_shapes=[pltpu.VMEM((LANE, LANE), F32),
                        pltpu.VMEM((8, LANE), F32), pltpu.VMEM((8, LANE), F32), pltpu.VMEM((8, LANE), F32),
                        pltpu.VMEM((8, 2 * LANE), F32), pltpu.VMEM((8, GATE_LORA), F32)]
                       + [half] * 9 + [full] * 4,
        compiler_params=_params(("parallel", "parallel", "arbitrary")),
        name="rwkv",
    )(p, p, p, p, p, *mus, w0, w2p, a0, a2p, g2, k_k, k_a, r_k, lnx_g, lnx_b, tri, e_seg)


def _gla_kernel(n_chunks, q_ref, k_ref, v_ref, og_ref, al_ref, wal_ref, bal_ref, ng_ref, o_ref, s_ref):
    @pl.when(pl.program_id(2) == 0)
    def _():
        s_ref[...] = jnp.zeros_like(s_ref)

    n_sub = CHUNK // SUB
    scale = GLA_DK ** -0.5
    rows = lax.broadcasted_iota(jnp.int32, (CHUNK, CHUNK), 0)
    cols = lax.broadcasted_iota(jnp.int32, (CHUNK, CHUNK), 1)
    tri = (rows >= cols).astype(BF16)
    srow = lax.broadcasted_iota(jnp.int32, (CHUNK, LANE), 0)
    slane = lax.broadcasted_iota(jnp.int32, (CHUNK, LANE), 1)

    def body(c, carry):
        sl = pl.ds(pl.multiple_of(c * CHUNK, CHUNK), CHUNK)
        q = q_ref[sl, :] * scale
        k = k_ref[sl, :]
        v = v_ref[sl, :]
        pre = _dot3("nn", al_ref[sl, :], wal_ref[...]) + bal_ref[...]
        g = jax.nn.log_sigmoid(pre) / GLA_TAU
        g_hi, g_lo = _split_bf16(g)
        b = _mm(tri, g_hi) + _mm(tri, g_lo)

        pieces = []
        for i in range(n_sub):
            lo, hi = i * SUB, (i + 1) * SUB
            base = b[lo - 1:lo, :] if i > 0 else jnp.zeros((1, LANE), F32)
            q_i = q[lo:hi, :] * jnp.exp(b[lo:hi, :] - base)
            if i > 0:
                k_hat = k[:lo, :] * jnp.exp(base - b[:lo, :])
                top = _bdot("nt", k_hat, q_i)
                pieces.append(jnp.concatenate([top, jnp.zeros((CHUNK - lo, SUB), F32)], axis=0))
            else:
                pieces.append(jnp.zeros((CHUNK, SUB), F32))
        pieces.append(jnp.zeros((CHUNK, LANE - CHUNK), F32))
        a_t = jnp.concatenate(pieces, axis=1)

        for tau in range(SUB):
            q_b = jnp.concatenate([jnp.broadcast_to(q[i * SUB + tau:i * SUB + tau + 1, :], (SUB, LANE))
                                   for i in range(n_sub)], axis=0)
            b_b = jnp.concatenate([jnp.broadcast_to(b[i * SUB + tau:i * SUB + tau + 1, :], (SUB, LANE))
                                   for i in range(n_sub)], axis=0)
            ok = (srow % SUB) <= tau
            dec = jnp.where(ok, jnp.exp(jnp.where(ok, b_b - b, 0.0)), 0.0)
            col = jnp.sum(k * q_b * dec, axis=-1, keepdims=True)
            hit = (slane % SUB == tau) & (slane // SUB == srow // SUB)
            a_t = a_t + jnp.where(hit, col, 0.0)

        s = s_ref[...]
        o = _bdot("tn", a_t[:, :CHUNK], v) + _bdot("nt", q * jnp.exp(b), s)
        b_last = b[CHUNK - 1:CHUNK, :]
        s_ref[...] = s * jnp.exp(b_last) + _bdot("tn", v, k * jnp.exp(b_last - b))

        o = o * lax.rsqrt(jnp.mean(o * o, axis=-1, keepdims=True) + NORM_EPS) * ng_ref[...]
        og = og_ref[sl, :]
        o_ref[sl, :] = (o * (og * jax.nn.sigmoid(og))).astype(o_ref.dtype)
        return carry

    lax.fori_loop(0, n_chunks, body, 0)


def _gla(p, w_alpha2p, b_alpha, norm_g, batch, seq, tb=256):
    t = p.shape[0]
    blocks_per_seq = seq // tb

    def col(width, base):
        return pl.BlockSpec((tb, width), lambda b, h, i: (b * blocks_per_seq + i, base // width + h))

    return pl.pallas_call(
        functools.partial(_gla_kernel, tb // CHUNK),
        grid=(batch, GLA_HEADS, blocks_per_seq),
        in_specs=[col(GLA_DK, COL_GQ), col(GLA_DK, COL_GK), col(GLA_DV, COL_GV), col(GLA_DV, COL_GOG),
                  pl.BlockSpec((tb, LANE), lambda b, h, i: (b * blocks_per_seq + i, COL_GAL // LANE)),
                  pl.BlockSpec((LANE, GLA_DK), lambda b, h, i: (0, h)),
                  pl.BlockSpec((1, GLA_DK), lambda b, h, i: (0, h)),
                  pl.BlockSpec((1, GLA_DV), lambda b, h, i: (0, 0))],
        out_specs=pl.BlockSpec((tb, GLA_DV), lambda b, h, i: (b * blocks_per_seq + i, h)),
        out_shape=jax.ShapeDtypeStruct((t, GLA_VAL_WIDTH), BF16),
        scratch_shapes=[pltpu.VMEM((GLA_DV, GLA_DK), F32)],
        compiler_params=_params(("parallel", "parallel", "arbitrary")),
        name="gla",
    )(p, p, p, p, p, w_alpha2p, b_alpha, norm_g)


def _merge_kernel(ya_ref, yb_ref, pg_ref, bm_ref, wa_ref, wb_ref, o_ref):
    d = wa_ref.shape[1]
    gates = jax.nn.sigmoid(pg_ref[...] + bm_ref[...])
    ma = _mm(ya_ref[...], wa_ref[...])
    mb = _mm(yb_ref[...], wb_ref[...])
    o_ref[...] = (gates[:, :d] * ma + gates[:, d:] * mb).astype(o_ref.dtype)


def _merge(ya, yb, p, b_merge, wa, wb, tm=256):
    t = ya.shape[0]
    d = wa.shape[1]
    return pl.pallas_call(
        _merge_kernel,
        grid=(t // tm,),
        in_specs=[pl.BlockSpec((tm, ya.shape[1]), lambda i: (i, 0)),
                  pl.BlockSpec((tm, yb.shape[1]), lambda i: (i, 0)),
                  pl.BlockSpec((tm, 2 * d), lambda i: (i, COL_GATE // (2 * d))),
                  pl.BlockSpec((1, 2 * d), lambda i: (0, 0)),
                  pl.BlockSpec(wa.shape, lambda i: (0, 0)),
                  pl.BlockSpec(wb.shape, lambda i: (0, 0))],
        out_specs=pl.BlockSpec((tm, d), lambda i: (i, 0)),
        out_shape=jax.ShapeDtypeStruct((t, d), BF16),
        compiler_params=_params(("parallel",)),
        name="merge",
    )(ya, yb, p, b_merge, wa, wb)


def _outproj_kernel(m_ref, x_ref, wo_ref, g_ref, wr_ref, br_ref, x1_ref, h_ref, lg_ref):
    x1 = x_ref[...] + _mm(m_ref[...], wo_ref[...])
    x1_ref[...] = x1
    ms = jnp.mean(x1 * x1, axis=-1, keepdims=True)
    h = x1 * lax.rsqrt(ms + NORM_EPS) * g_ref[...]
    h_ref[...] = h
    lg_ref[...] = _mm(h, wr_ref[...], HI) + br_ref[...]


def _outproj(merged, x2d, w_out, g_ffn, w_router_p, b_router_p, tm=256):
    t, d = x2d.shape
    row = pl.BlockSpec((tm, d), lambda i: (i, 0))
    return pl.pallas_call(
        _outproj_kernel,
        grid=(t // tm,),
        in_specs=[row, row,
                  pl.BlockSpec((d, d), lambda i: (0, 0)),
                  pl.BlockSpec((1, d), lambda i: (0, 0)),
                  pl.BlockSpec((d, LANE), lambda i: (0, 0)),
                  pl.BlockSpec((1, LANE), lambda i: (0, 0))],
        out_specs=[row, row, pl.BlockSpec((tm, LANE), lambda i: (i, 0))],
        out_shape=[jax.ShapeDtypeStruct((t, d), F32), jax.ShapeDtypeStruct((t, d), F32),
                   jax.ShapeDtypeStruct((t, LANE), F32)],
        compiler_params=_params(("parallel",)),
        name="outproj",
    )(merged, x2d, w_out, g_ffn, w_router_p, b_router_p)


def _row_copy(src_ref, src_row, dst_ref, dst_row, sem):
    return pltpu.make_async_copy(src_ref.at[pl.ds(src_row, 1)], dst_ref.at[pl.ds(dst_row, 1)], sem)


def _dispatch_kernel(tm, dest_ref, h_ref, xs_in_ref, xs_ref, sem):
    del xs_in_ref

    def issue(r, carry):
        for j in range(TOP_K):
            _row_copy(h_ref, r, xs_ref, dest_ref[0, 0, r * TOP_K + j], sem).start()
        return carry

    lax.fori_loop(0, tm, issue, 0)

    def drain(r, carry):
        for j in range(TOP_K):
            _row_copy(h_ref, r, xs_ref, dest_ref[0, 0, r * TOP_K + j], sem).wait()
        return carry

    lax.fori_loop(0, tm, drain, 0)


def _dispatch(h, dest, n_rows, tm=256):
    t, d = h.shape
    xs0 = jnp.zeros((n_rows, d), h.dtype)
    dest3 = dest.reshape(t // tm, 1, tm * TOP_K)
    return pl.pallas_call(
        functools.partial(_dispatch_kernel, tm),
        grid=(t // tm,),
        in_specs=[pl.BlockSpec((1, 1, tm * TOP_K), lambda i: (i, 0, 0), memory_space=pltpu.SMEM),
                  pl.BlockSpec((tm, d), lambda i: (i, 0)),
                  pl.BlockSpec(memory_space=pl.ANY)],
        out_specs=pl.BlockSpec(memory_space=pl.ANY),
        out_shape=jax.ShapeDtypeStruct((n_rows, d), h.dtype),
        scratch_shapes=[pltpu.SemaphoreType.DMA(())],
        input_output_aliases={2: 0},
        compiler_params=_params(("arbitrary",)),
        name="dispatch",
    )(dest3, h, xs0)


MOE_TILE = 1024
MOE_SUB = 256


def _expert_kernel(te_ref, ns_ref, nv_ref, x_ref, wg_ref, bg_ref, wu_ref, bu_ref, wd_ref, bd_ref, o_ref, xb_ref):
    w = pl.program_id(0)
    f = pl.program_id(1)
    n_sub = ns_ref[w]

    @pl.when(f == 0)
    def _():
        xb_ref[...] = x_ref[...].astype(BF16)
        o_ref[...] = jnp.broadcast_to(bd_ref[0], o_ref.shape)

    def sub(i, carry):
        rows = pl.ds(pl.multiple_of(i * MOE_SUB, MOE_SUB), MOE_SUB)
        xb = xb_ref[rows, :]
        gate = _mm(xb, wg_ref[0].astype(BF16)) + bg_ref[0]
        up = _mm(xb, wu_ref[0].astype(BF16)) + bu_ref[0]
        gate = jnp.minimum(gate, SWIGLU_LIMIT)
        up = jnp.clip(up, -SWIGLU_LIMIT, SWIGLU_LIMIT)
        glu = gate * jax.nn.sigmoid(gate * SWIGLU_ALPHA)
        act = ((up + 1.0) * glu).astype(BF16)
        o_ref[rows, :] += _mm(act, wd_ref[0].astype(BF16))
        return carry

    lax.fori_loop(0, n_sub, sub, 0)


def _experts(xs, tile_expert, tile_nsub, n_valid, w_gate, b_gate, w_up, b_up, w_down, b_down, tf=256):
    n_rows, d = xs.shape
    n_e, _, d_ff = w_gate.shape
    tm = MOE_TILE
    n_tiles = n_rows // tm
    n_f = d_ff // tf

    def tile(w, nv):
        return jnp.minimum(w, nv[0] - 1)

    def fidx(w, f, ns):
        return jnp.where(ns[w] > 0, f, n_f - 1)

    grid_spec = pltpu.PrefetchScalarGridSpec(
        num_scalar_prefetch=3,
        grid=(n_tiles, n_f),
        in_specs=[pl.BlockSpec((tm, d), lambda w, f, te, ns, nv: (tile(w, nv), 0)),
                  pl.BlockSpec((1, d, tf), lambda w, f, te, ns, nv: (te[w], 0, fidx(w, f, ns))),
                  pl.BlockSpec((1, 1, tf), lambda w, f, te, ns, nv: (te[w], 0, fidx(w, f, ns))),
                  pl.BlockSpec((1, d, tf), lambda w, f, te, ns, nv: (te[w], 0, fidx(w, f, ns))),
                  pl.BlockSpec((1, 1, tf), lambda w, f, te, ns, nv: (te[w], 0, fidx(w, f, ns))),
                  pl.BlockSpec((1, tf, d), lambda w, f, te, ns, nv: (te[w], fidx(w, f, ns), 0)),
                  pl.BlockSpec((1, 1, d), lambda w, f, te, ns, nv: (te[w], 0, 0))],
        out_specs=pl.BlockSpec((tm, d), lambda w, f, te, ns, nv: (w, 0)),
        scratch_shapes=[pltpu.VMEM((tm, d), BF16)],
    )
    return pl.pallas_call(
        _expert_kernel,
        grid_spec=grid_spec,
        out_shape=jax.ShapeDtypeStruct((n_rows, d), F32),
        compiler_params=_params(("arbitrary", "arbitrary")),
        name="experts",
    )(tile_expert, tile_nsub, n_valid, xs, w_gate, b_gate.reshape(n_e, 1, d_ff), w_up,
      b_up.reshape(n_e, 1, d_ff), w_down, b_down.reshape(n_e, 1, d))


def _combine_kernel(tm, final, pos_ref, eo_ref, x1_ref, wt_ref, g_ref, o_ref, buf_ref, sem):
    def issue(r, carry):
        for j in range(TOP_K):
            _row_copy(eo_ref, pos_ref[0, 0, r * TOP_K + j], buf_ref.at[j], r, sem).start()
        return carry

    lax.fori_loop(0, tm, issue, 0)

    def drain(r, carry):
        for j in range(TOP_K):
            _row_copy(eo_ref, pos_ref[0, 0, r * TOP_K + j], buf_ref.at[j], r, sem).wait()
        return carry

    lax.fori_loop(0, tm, drain, 0)

    wt = wt_ref[...]
    y = x1_ref[...]
    for j in range(TOP_K):
        y = y + buf_ref[j] * wt[:, j:j + 1]
    if final:
        y = y * lax.rsqrt(jnp.mean(y * y, axis=-1, keepdims=True) + NORM_EPS) * g_ref[...]
    o_ref[...] = y


def _combine(eo, pos, x1, top_w, g_final, final, tm=128):
    t, d = x1.shape
    pos3 = pos.reshape(t // tm, 1, tm * TOP_K)
    return pl.pallas_call(
        functools.partial(_combine_kernel, tm, final),
        grid=(t // tm,),
        in_specs=[pl.BlockSpec((1, 1, tm * TOP_K), lambda i: (i, 0, 0), memory_space=pltpu.SMEM),
                  pl.BlockSpec(memory_space=pl.ANY),
                  pl.BlockSpec((tm, d), lambda i: (i, 0)),
                  pl.BlockSpec((tm, TOP_K), lambda i: (i, 0)),
                  pl.BlockSpec((1, d), lambda i: (0, 0))],
        out_specs=pl.BlockSpec((tm, d), lambda i: (i, 0)),
        out_shape=jax.ShapeDtypeStruct((t, d), F32),
        scratch_shapes=[pltpu.VMEM((TOP_K, tm, d), F32), pltpu.SemaphoreType.DMA(())],
        compiler_params=_params(("arbitrary",)),
        name="combine",
    )(pos3, eo, x1, top_w, g_final)


def _routing(logits):
    t = logits.shape[0]
    tm = MOE_TILE
    top_vals, top_idx = lax.top_k(logits[:, :N_EXPERTS], TOP_K)
    top_w = jax.nn.softmax(top_vals, axis=-1)
    e_flat = top_idx.reshape(-1).astype(jnp.int32)
    onehot = (e_flat[:, None] == jnp.arange(N_EXPERTS, dtype=jnp.int32)[None, :]).astype(jnp.int32)
    incl = jnp.cumsum(onehot, axis=0)
    counts = incl[-1]
    rank = jnp.sum((incl - onehot) * onehot, axis=1)
    tiles = (counts + tm - 1) // tm
    tile_end = jnp.cumsum(tiles)
    tile_start = tile_end - tiles
    dest = jnp.sum(onehot * (tile_start * tm)[None, :], axis=1) + rank
    n_tiles = (t * TOP_K) // tm + N_EXPERTS
    wid = jnp.arange(n_tiles, dtype=jnp.int32)
    n_valid = tile_end[-1:].astype(jnp.int32)
    tile_expert = jnp.minimum(jnp.searchsorted(tile_end, jnp.minimum(wid, n_valid[0] - 1), side="right"),
                              N_EXPERTS - 1).astype(jnp.int32)
    tile_rows = jnp.clip(counts[tile_expert] - (wid - tile_start[tile_expert]) * tm, 0, tm)
    tile_nsub = jnp.where(wid < n_valid[0], (tile_rows + MOE_SUB - 1) // MOE_SUB, 0).astype(jnp.int32)
    return top_w, dest.astype(jnp.int32), tile_expert, tile_nsub, n_valid, n_tiles * tm


def _pad_cols(a, width):
    return jnp.pad(a, ((0, 0), (0, width - a.shape[1])))


def _pad_rows(a, height):
    return jnp.pad(a, ((0, height - a.shape[0]), (0, 0)))


def _split_cols(a, sizes):
    out, acc = [], 0
    for s in sizes:
        out.append(a[..., acc:acc + s])
        acc += s
    return out


def kernel(x, norm_mix_g, w_in, token_shift_mu, rwkv_w0, rwkv_w2, rwkv_a0, rwkv_a2, rwkv_g2, rwkv_k_k, rwkv_k_a, rwkv_r_k, rwkv_lnx_g, rwkv_lnx_b, gla_w_alpha2, gla_b_alpha, gla_norm_g, b_merge, w_branch_rwkv, w_branch_gla, w_out, norm_ffn_g, w_router, b_router, w_gate, b_gate, w_up, b_up, w_down, b_down, norm_final_g):
    batch, seq, d = x.shape
    t = batch * seq
    depth = w_in.shape[0]
    w = RWKV_WIDTH
    sizes = (w, w, w, DECAY_LORA, ICL_LORA, GATE_LORA, GLA_KEY_WIDTH, GLA_KEY_WIDTH, GLA_VAL_WIDTH,
             GLA_VAL_WIDTH, GLA_GATE_LORA, 2 * d)

    x2d = x.reshape(t, d)
    for l in range(depth):
        s_r, s_k, s_v, s_wl, s_al, s_gl, s_gq, s_gk, s_gv, s_gog, s_gal, s_gate = _split_cols(w_in[l], sizes)
        w_p = jnp.concatenate(
            [s_gate, s_r, s_k, s_v, _pad_cols(s_wl, LANE), _pad_cols(s_al, LANE), s_gl, s_gq, s_gk, s_gv,
             s_gog, _pad_cols(s_gal, LANE)], axis=1).astype(BF16)
        mu = token_shift_mu[l][None, :]
        m_r, m_k, m_v, m_wl, m_al, m_gl = _split_cols(mu, sizes[:6])
        mus = (m_r, m_k, m_v, jnp.concatenate([_pad_cols(m_wl, LANE), _pad_cols(m_al, LANE)], axis=1), m_gl)

        p = _inproj(x2d, norm_mix_g[l][None, :], w_p)

        ya = _rwkv(p, batch, seq, mus, rwkv_w0[l][None, :], _pad_rows(rwkv_w2[l], LANE), rwkv_a0[l][None, :],
                   _pad_rows(rwkv_a2[l], LANE), rwkv_g2[l], rwkv_k_k[l][None, :], rwkv_k_a[l][None, :],
                   rwkv_r_k[l].reshape(1, w), rwkv_lnx_g[l][None, :], rwkv_lnx_b[l][None, :])

        yb = _gla(p, _pad_rows(gla_w_alpha2[l], LANE), gla_b_alpha[l][None, :], gla_norm_g[l][None, :],
                  batch, seq)

        merged = _merge(ya, yb, p, b_merge[l][None, :], w_branch_rwkv[l].astype(BF16),
                        w_branch_gla[l].astype(BF16))
        b_router_p = jnp.concatenate([b_router[l], jnp.full((LANE - N_EXPERTS,), -1e30, F32)])[None, :]
        x1, h2, logits = _outproj(merged, x2d, w_out[l].astype(BF16), norm_ffn_g[l][None, :],
                                  _pad_cols(w_router[l], LANE), b_router_p)

        top_w, dest, tile_expert, tile_nsub, n_valid, n_rows = _routing(logits)
        xs = _dispatch(h2, dest, n_rows)
        eo = _experts(xs, tile_expert, tile_nsub, n_valid, w_gate[l], b_gate[l], w_up[l], b_up[l],
                      w_down[l], b_down[l])
        final = l == depth - 1
        x2d = _combine(eo, dest, x1, top_w, norm_final_g[None, :], final)
    return x2d.reshape(batch, seq, d)
```

```python
import functools

import jax
import jax.numpy as jnp
from jax import lax
from jax.experimental import pallas as pl
from jax.experimental.pallas import tpu as pltpu

F32 = jnp.float32
BF16 = jnp.bfloat16
HI = lax.Precision.HIGHEST

NORM_EPS = 1e-5
RWKV_GN_EPS = 64e-5
RWKV_WIDTH = 1024
RWKV_HEAD = 64
DECAY_LORA = 96
ICL_LORA = 96
GATE_LORA = 256
GLA_HEADS = 4
GLA_DK = 128
GLA_DV = 256
GLA_KEY_WIDTH = GLA_HEADS * GLA_DK
GLA_VAL_WIDTH = GLA_HEADS * GLA_DV
GLA_GATE_LORA = 16
GLA_TAU = 16.0
N_EXPERTS = 32
TOP_K = 4
SWIGLU_LIMIT = 7.0
SWIGLU_ALPHA = 1.702

CHUNK = 64
SUB = 16
LANE = 128

COL_GATE = 0
COL_R = 4096
COL_K = 5120
COL_V = 6144
COL_WA = 7168
COL_GL = 7424
COL_GQ = 7680
COL_GK = 8192
COL_GV = 8704
COL_GOG = 9728
COL_GAL = 10752
N_PROJ = 10880

VMEM_LIMIT = 56 * 1024 * 1024

_DIMS = {"nn": (((1,), (0,)), ((), ())), "nt": (((1,), (1,)), ((), ())), "tn": (((0,), (0,)), ((), ()))}


def _mm(a, b, prec=None):
    return jnp.dot(a, b, preferred_element_type=F32, precision=prec)


def _bdot(kind, a, b):
    return lax.dot_general(a.astype(BF16), b.astype(BF16), _DIMS[kind], preferred_element_type=F32)


def _split_bf16(a):
    hi = a.astype(BF16)
    return hi, (a - hi.astype(F32)).astype(BF16)


def _dot3(kind, a, b):
    ah, al = _split_bf16(a)
    bh, bl = _split_bf16(b)
    return _bdot(kind, ah, bh) + _bdot(kind, ah, bl) + _bdot(kind, al, bh)


def _seg_sum(x, e_ref):
    hi, lo = _split_bf16(x)
    e = e_ref[...]
    return _mm(hi, e) + _mm(lo, e)


def _params(sem, vmem=VMEM_LIMIT):
    return pltpu.CompilerParams(dimension_semantics=sem, vmem_limit_bytes=vmem)


def _inproj_kernel(x_ref, g_ref, w_ref, o_ref, h_ref):
    @pl.when(pl.program_id(1) == 0)
    def _():
        x = x_ref[...]
        ms = jnp.mean(x * x, axis=-1, keepdims=True)
        h_ref[...] = (x * lax.rsqrt(ms + NORM_EPS) * g_ref[...]).astype(BF16)

    o_ref[...] = _mm(h_ref[...], w_ref[...])


def _inproj(x2d, g, w_p, tm=512, tn=640):
    t, d = x2d.shape
    n = w_p.shape[1]
    return pl.pallas_call(
        _inproj_kernel,
        grid=(t // tm, n // tn),
        in_specs=[pl.BlockSpec((tm, d), lambda i, j: (i, 0)),
                  pl.BlockSpec((1, d), lambda i, j: (0, 0)),
                  pl.BlockSpec((d, tn), lambda i, j: (0, j))],
        out_specs=pl.BlockSpec((tm, tn), lambda i, j: (i, j)),
        out_shape=jax.ShapeDtypeStruct((t, n), F32),
        scratch_shapes=[pltpu.VMEM((tm, d), BF16)],
        compiler_params=_params(("parallel", "arbitrary")),
        name="inproj",
    )(x2d, g, w_p)


def _softplus(y):
    return jnp.maximum(y, 0.0) + jnp.log(1.0 + jnp.exp(-jnp.abs(y)))


def _rwkv_kernel(tb, r_ref, k_ref, v_ref, wa_ref, gl_ref,
                 mur_ref, muk_ref, muv_ref, muwa_ref, mugl_ref,
                 w0_ref, w2_ref, a0_ref, a2_ref, g2_ref, kk_ref, ka_ref, rk_ref, lg_ref, lb_ref,
                 tri_ref, e_ref, o_ref,
                 s_ref, pr_ref, pk_ref, pv_ref, pwa_ref, pgl_ref,
                 ah_ref, al_ref, bh_ref, bl_ref, kh_ref, kl_ref, rh_ref, rl_ref, vb_ref,
                 cum_ref, y_ref, bonus_ref, g_ref):
    @pl.when(pl.program_id(2) == 0)
    def _():
        s_ref[...] = jnp.zeros_like(s_ref)
        for ref in (pr_ref, pk_ref, pv_ref, pwa_ref, pgl_ref):
            ref[...] = jnp.zeros_like(ref)

    def shift_mix(cur_ref, prev_ref, mu_ref):
        cur = cur_ref[...]
        row = lax.broadcasted_iota(jnp.int32, cur.shape, 0)
        shifted = jnp.where(row == 0, prev_ref[0:1, :], pltpu.roll(cur, 1, axis=0))
        prev_ref[0:1, :] = cur[tb - 1:tb, :]
        return cur + (shifted - cur) * mu_ref[...]

    r = shift_mix(r_ref, pr_ref, mur_ref)
    k = shift_mix(k_ref, pk_ref, muk_ref)
    v = shift_mix(v_ref, pv_ref, muv_ref)
    wa = shift_mix(wa_ref, pwa_ref, muwa_ref)
    gl = shift_mix(gl_ref, pgl_ref, mugl_ref)

    z = w0_ref[...] + _dot3("nn", jnp.tanh(wa[:, :LANE]), w2_ref[...])
    logw = -jnp.exp(-_softplus(-z) - 0.5)
    a = jax.nn.sigmoid(a0_ref[...] + _dot3("nn", wa[:, LANE:], a2_ref[...]))
    g_ref[...] = _bdot("nn", jax.nn.sigmoid(gl), g2_ref[...])

    kk = k * kk_ref[...]
    kk = kk / jnp.maximum(jnp.sqrt(_seg_sum(kk * kk, e_ref)), 1e-12)
    k2 = k * (1.0 + (a - 1.0) * ka_ref[...])
    bonus_ref[...] = _seg_sum(r * k2 * rk_ref[...], e_ref) * v

    lw_hi, lw_lo = _split_bf16(logw)
    cum = _mm(tri_ref[...], lw_hi) + _mm(tri_ref[...], lw_lo)
    inv = jnp.exp(-cum)
    cum_ref[...] = cum
    for val, hi_ref, lo_ref in ((-kk * jnp.exp(cum - logw), ah_ref, al_ref), (kk * a * inv, bh_ref, bl_ref),
                                (k2 * inv, kh_ref, kl_ref), (r * jnp.exp(cum), rh_ref, rl_ref)):
        hi, lo = _split_bf16(val)
        hi_ref[...] = hi
        lo_ref[...] = lo
    vb_ref[...] = v.astype(BF16)

    c2 = 2 * CHUNK
    lane = lax.broadcasted_iota(jnp.int32, (CHUNK, LANE), 1)
    head0 = lane < RWKV_HEAD
    rows = lax.broadcasted_iota(jnp.int32, (c2, c2), 0)
    cols = lax.broadcasted_iota(jnp.int32, (c2, c2), 1)
    strict = rows > cols
    incl = rows >= cols
    eye = (rows == cols).astype(F32)
    zero = jnp.zeros((CHUNK, LANE), BF16)

    def stack(ref, c):
        x = ref[c * CHUNK:(c + 1) * CHUNK, :]
        return jnp.concatenate([jnp.where(head0, x, zero), jnp.where(head0, zero, x)], axis=0)

    chunks = range(tb // CHUNK)
    a_h, b_h, k_h, r_h, v_s = ([stack(ref, c) for c in chunks] for ref in (ah_ref, bh_ref, kh_ref, rh_ref, vb_ref))
    a_l, b_l, k_l, r_l = ([stack(ref, c) for c in chunks] for ref in (al_ref, bl_ref, kl_ref, rl_ref))
    ar_h = [jnp.concatenate([a_h[c], r_h[c]], axis=0) for c in chunks]
    bk_h = [jnp.concatenate([b_h[c], k_h[c]], axis=0) for c in chunks]
    gram = [_bdot("nt", ar_h[c], bk_h[c]) + _bdot("nt", ar_h[c], jnp.concatenate([b_l[c], k_l[c]], axis=0))
            + _bdot("nt", jnp.concatenate([a_l[c], r_l[c]], axis=0), bk_h[c]) for c in chunks]
    l_ba = [jnp.where(strict, gram[c][:c2, :c2], 0.0) for c in chunks]
    l_ka = [jnp.where(strict, gram[c][:c2, c2:], 0.0).astype(BF16) for c in chunks]
    g_b = [jnp.where(incl, gram[c][c2:, :c2], 0.0).astype(BF16) for c in chunks]
    g_k = [jnp.where(incl, gram[c][c2:, c2:], 0.0).astype(BF16) for c in chunks]

    t_inv = [eye + l_ba[c] for c in chunks]
    x = [l_ba[c].astype(BF16) for c in chunks]
    step = 2
    while step < CHUNK:
        x = [_bdot("nn", x[c], x[c]).astype(BF16) for c in chunks]
        t_inv = [t_inv[c] + _bdot("nn", t_inv[c], x[c]) for c in chunks]
        step *= 2

    w2 = [_bdot("nn", l_ka[c], v_s[c]).astype(BF16) for c in chunks]
    p12 = [_bdot("nn", t_inv[c], jnp.concatenate([a_h[c], w2[c]], axis=1)).astype(BF16) for c in chunks]
    qz = [_bdot("nn", g_b[c], p12[c]) for c in chunks]
    q = [(r_h[c].astype(F32) + r_l[c].astype(F32) + qz[c][:, :LANE]).astype(BF16) for c in chunks]
    z_c = [qz[c][:, LANE:] + _bdot("nn", g_k[c], v_s[c]) for c in chunks]
    mn = [_bdot("tn", p12[c], b_h[c]) for c in chunks]
    vk = [_bdot("tn", v_s[c], k_h[c]) for c in chunks]

    s = s_ref[...]
    for c in chunks:
        g_end = jnp.exp(cum_ref[(c + 1) * CHUNK - 1:(c + 1) * CHUNK, :])
        y_st = _bdot("nt", q[c], s) + z_c[c]
        y_ref[c * CHUNK:(c + 1) * CHUNK, :] = y_st[:CHUNK] + y_st[CHUNK:]
        m = (eye + mn[c][:LANE]) * g_end
        n = (mn[c][LANE:] + vk[c]) * g_end
        s = _bdot("nn", s, m) + n
    s_ref[...] = s

    y = y_ref[...]
    inv_n = 1.0 / RWKV_HEAD
    mean = _seg_sum(y, e_ref) * inv_n
    d = y - mean
    var = _seg_sum(d * d, e_ref) * inv_n
    yn = d * lax.rsqrt(var + RWKV_GN_EPS) * lg_ref[...] + lb_ref[...]
    o_ref[...] = ((yn + bonus_ref[...]) * g_ref[...]).astype(o_ref.dtype)


def _rwkv(p, batch, seq, mus, w0, w2p, a0, a2p, g2, k_k, k_a, r_k, lnx_g, lnx_b, tb=512):
    t = p.shape[0]
    w = RWKV_WIDTH
    blocks_per_seq = seq // tb

    def seg(width, base, per_head):
        return pl.BlockSpec((tb, width),
                            lambda b, h, i: (b * blocks_per_seq + i, base // width + (h if per_head else 0)))

    def vec(width, per_head=True):
        return pl.BlockSpec((1, width), lambda b, h, i: (0, h if per_head else 0))

    def mat(rows_):
        return pl.BlockSpec((rows_, LANE), lambda b, h, i: (0, h))

    rows = lax.broadcasted_iota(jnp.int32, (tb, tb), 0)
    cols = lax.broadcasted_iota(jnp.int32, (tb, tb), 1)
    tri = ((rows // CHUNK == cols // CHUNK) & (rows >= cols)).astype(BF16)
    rows = lax.broadcasted_iota(jnp.int32, (LANE, LANE), 0)
    cols = lax.broadcasted_iota(jnp.int32, (LANE, LANE), 1)
    e_seg = (rows // RWKV_HEAD == cols // RWKV_HEAD).astype(BF16)

    in_specs = [seg(LANE, COL_R, True), seg(LANE, COL_K, True), seg(LANE, COL_V, True),
                seg(2 * LANE, COL_WA, False), seg(GATE_LORA, COL_GL, False),
                vec(LANE), vec(LANE), vec(LANE), vec(2 * LANE, False), vec(GATE_LORA, False),
                vec(LANE), mat(LANE), vec(LANE), mat(LANE), mat(GATE_LORA), vec(LANE), vec(LANE), vec(LANE),
                vec(LANE), vec(LANE),
                pl.BlockSpec((tb, tb), lambda b, h, i: (0, 0)),
                pl.BlockSpec((LANE, LANE), lambda b, h, i: (0, 0))]
    half = pltpu.VMEM((tb, LANE), BF16)
    full = pltpu.VMEM((tb, LANE), F32)
    return pl.pallas_call(
        functools.partial(_rwkv_kernel, tb),
        grid=(batch, w // LANE, blocks_per_seq),
        in_specs=in_specs,
        out_specs=pl.BlockSpec((tb, LANE), lambda b, h, i: (b * blocks_per_seq + i, h)),
        out_shape=jax.ShapeDtypeStruct((t, w), BF16),
        scratch_shapes=[pltpu.VMEM((LANE, LANE), F32),
                        pltpu.VMEM((8, LANE), F32), pltpu.VMEM((8, LANE), F32), pltpu.VMEM((8, LANE), F32),
                        pltpu.VMEM((8, 2 * LANE), F32), pltpu.VMEM((8, GATE_LORA), F32)]
                       + [half] * 9 + [full] * 4,
        compiler_params=_params(("parallel", "parallel", "arbitrary")),
        name="rwkv",
    )(p, p, p, p, p, *mus, w0, w2p, a0, a2p, g2, k_k, k_a, r_k, lnx_g, lnx_b, tri, e_seg)


def _gla_kernel(n_chunks, q_ref, k_ref, v_ref, og_ref, al_ref, wal_ref, bal_ref, ng_ref, o_ref, s_ref):
    @pl.when(pl.program_id(2) == 0)
    def _():
        s_ref[...] = jnp.zeros_like(s_ref)

    n_sub = CHUNK // SUB
    scale = GLA_DK ** -0.5
    rows = lax.broadcasted_iota(jnp.int32, (CHUNK, CHUNK), 0)
    cols = lax.broadcasted_iota(jnp.int32, (CHUNK, CHUNK), 1)
    tri = (rows >= cols).astype(BF16)
    srow = lax.broadcasted_iota(jnp.int32, (CHUNK, LANE), 0)
    slane = lax.broadcasted_iota(jnp.int32, (CHUNK, LANE), 1)

    def body(c, carry):
        sl = pl.ds(pl.multiple_of(c * CHUNK, CHUNK), CHUNK)
        q = q_ref[sl, :] * scale
        k = k_ref[sl, :]
        v = v_ref[sl, :]
        pre = _dot3("nn", al_ref[sl, :], wal_ref[...]) + bal_ref[...]
        g = jax.nn.log_sigmoid(pre) / GLA_TAU
        g_hi, g_lo = _split_bf16(g)
        b = _mm(tri, g_hi) + _mm(tri, g_lo)

        pieces = []
        for i in range(n_sub):
            lo, hi = i * SUB, (i + 1) * SUB
            base = b[lo - 1:lo, :] if i > 0 else jnp.zeros((1, LANE), F32)
            q_i = q[lo:hi, :] * jnp.exp(b[lo:hi, :] - base)
            if i > 0:
                k_hat = k[:lo, :] * jnp.exp(base - b[:lo, :])
                top = _bdot("nt", k_hat, q_i)
                pieces.append(jnp.concatenate([top, jnp.zeros((CHUNK - lo, SUB), F32)], axis=0))
            else:
                pieces.append(jnp.zeros((CHUNK, SUB), F32))
        pieces.append(jnp.zeros((CHUNK, LANE - CHUNK), F32))
        a_t = jnp.concatenate(pieces, axis=1)

        for tau in range(SUB):
            q_b = jnp.concatenate([jnp.broadcast_to(q[i * SUB + tau:i * SUB + tau + 1, :], (SUB, LANE))
                                   for i in range(n_sub)], axis=0)
            b_b = jnp.concatenate([jnp.broadcast_to(b[i * SUB + tau:i * SUB + tau + 1, :], (SUB, LANE))
                                   for i in range(n_sub)], axis=0)
            ok = (srow % SUB) <= tau
            dec = jnp.where(ok, jnp.exp(jnp.where(ok, b_b - b, 0.0)), 0.0)
            col = jnp.sum(k * q_b * dec, axis=-1, keepdims=True)
            hit = (slane % SUB == tau) & (slane // SUB == srow // SUB)
            a_t = a_t + jnp.where(hit, col, 0.0)

        s = s_ref[...]
        o = _bdot("tn", a_t[:, :CHUNK], v) + _bdot("nt", q * jnp.exp(b), s)
        b_last = b[CHUNK - 1:CHUNK, :]
        s_ref[...] = s * jnp.exp(b_last) + _bdot("tn", v, k * jnp.exp(b_last - b))

        o = o * lax.rsqrt(jnp.mean(o * o, axis=-1, keepdims=True) + NORM_EPS) * ng_ref[...]
        og = og_ref[sl, :]
        o_ref[sl, :] = (o * (og * jax.nn.sigmoid(og))).astype(o_ref.dtype)
        return carry

    lax.fori_loop(0, n_chunks, body, 0)


def _gla(p, w_alpha2p, b_alpha, norm_g, batch, seq, tb=256):
    t = p.shape[0]
    blocks_per_seq = seq // tb

    def col(width, base):
        return pl.BlockSpec((tb, width), lambda b, h, i: (b * blocks_per_seq + i, base // width + h))

    return pl.pallas_call(
        functools.partial(_gla_kernel, tb // CHUNK),
        grid=(batch, GLA_HEADS, blocks_per_seq),
        in_specs=[col(GLA_DK, COL_GQ), col(GLA_DK, COL_GK), col(GLA_DV, COL_GV), col(GLA_DV, COL_GOG),
                  pl.BlockSpec((tb, LANE), lambda b, h, i: (b * blocks_per_seq + i, COL_GAL // LANE)),
                  pl.BlockSpec((LANE, GLA_DK), lambda b, h, i: (0, h)),
                  pl.BlockSpec((1, GLA_DK), lambda b, h, i: (0, h)),
                  pl.BlockSpec((1, GLA_DV), lambda b, h, i: (0, 0))],
        out_specs=pl.BlockSpec((tb, GLA_DV), lambda b, h, i: (b * blocks_per_seq + i, h)),
        out_shape=jax.ShapeDtypeStruct((t, GLA_VAL_WIDTH), BF16),
        scratch_shapes=[pltpu.VMEM((GLA_DV, GLA_DK), F32)],
        compiler_params=_params(("parallel", "parallel", "arbitrary")),
        name="gla",
    )(p, p, p, p, p, w_alpha2p, b_alpha, norm_g)


def _merge_kernel(ya_ref, yb_ref, pg_ref, bm_ref, wa_ref, wb_ref, o_ref):
    d = wa_ref.shape[1]
    gates = jax.nn.sigmoid(pg_ref[...] + bm_ref[...])
    ma = _mm(ya_ref[...], wa_ref[...])
    mb = _mm(yb_ref[...], wb_ref[...])
    o_ref[...] = (gates[:, :d] * ma + gates[:, d:] * mb).astype(o_ref.dtype)


def _merge(ya, yb, p, b_merge, wa, wb, tm=256):
    t = ya.shape[0]
    d = wa.shape[1]
    return pl.pallas_call(
        _merge_kernel,
        grid=(t // tm,),
        in_specs=[pl.BlockSpec((tm, ya.shape[1]), lambda i: (i, 0)),
                  pl.BlockSpec((tm, yb.shape[1]), lambda i: (i, 0)),
                  pl.BlockSpec((tm, 2 * d), lambda i: (i, COL_GATE // (2 * d))),
                  pl.BlockSpec((1, 2 * d), lambda i: (0, 0)),
                  pl.BlockSpec(wa.shape, lambda i: (0, 0)),
                  pl.BlockSpec(wb.shape, lambda i: (0, 0))],
        out_specs=pl.BlockSpec((tm, d), lambda i: (i, 0)),
        out_shape=jax.ShapeDtypeStruct((t, d), BF16),
        compiler_params=_params(("parallel",)),
        name="merge",
    )(ya, yb, p, b_merge, wa, wb)


def _outproj_kernel(m_ref, x_ref, wo_ref, g_ref, wr_ref, br_ref, x1_ref, h_ref, lg_ref):
    x1 = x_ref[...] + _mm(m_ref[...], wo_ref[...])
    x1_ref[...] = x1
    ms = jnp.mean(x1 * x1, axis=-1, keepdims=True)
    h = x1 * lax.rsqrt(ms + NORM_EPS) * g_ref[...]
    h_ref[...] = h
    lg_ref[...] = _mm(h, wr_ref[...], HI) + br_ref[...]


def _outproj(merged, x2d, w_out, g_ffn, w_router_p, b_router_p, tm=256):
    t, d = x2d.shape
    row = pl.BlockSpec((tm, d), lambda i: (i, 0))
    return pl.pallas_call(
        _outproj_kernel,
        grid=(t // tm,),
        in_specs=[row, row,
                  pl.BlockSpec((d, d), lambda i: (0, 0)),
                  pl.BlockSpec((1, d), lambda i: (0, 0)),
                  pl.BlockSpec((d, LANE), lambda i: (0, 0)),
                  pl.BlockSpec((1, LANE), lambda i: (0, 0))],
        out_specs=[row, row, pl.BlockSpec((tm, LANE), lambda i: (i, 0))],
        out_shape=[jax.ShapeDtypeStruct((t, d), F32), jax.ShapeDtypeStruct((t, d), F32),
                   jax.ShapeDtypeStruct((t, LANE), F32)],
        compiler_params=_params(("parallel",)),
        name="outproj",
    )(merged, x2d, w_out, g_ffn, w_router_p, b_router_p)


def _row_copy(src_ref, src_row, dst_ref, dst_row, sem):
    return pltpu.make_async_copy(src_ref.at[pl.ds(src_row, 1)], dst_ref.at[pl.ds(dst_row, 1)], sem)


def _dispatch_kernel(tm, dest_ref, h_ref, xs_in_ref, xs_ref, sem):
    del xs_in_ref

    def issue(r, carry):
        for j in range(TOP_K):
            _row_copy(h_ref, r, xs_ref, dest_ref[0, 0, r * TOP_K + j], sem).start()
        return carry

    lax.fori_loop(0, tm, issue, 0)

    def drain(r, carry):
        for j in range(TOP_K):
            _row_copy(h_ref, r, xs_ref, dest_ref[0, 0, r * TOP_K + j], sem).wait()
        return carry

    lax.fori_loop(0, tm, drain, 0)


def _dispatch(h, dest, n_rows, tm=256):
    t, d = h.shape
    xs0 = jnp.zeros((n_rows, d), h.dtype)
    dest3 = dest.reshape(t // tm, 1, tm * TOP_K)
    return pl.pallas_call(
        functools.partial(_dispatch_kernel, tm),
        grid=(t // tm,),
        in_specs=[pl.BlockSpec((1, 1, tm * TOP_K), lambda i: (i, 0, 0), memory_space=pltpu.SMEM),
                  pl.BlockSpec((tm, d), lambda i: (i, 0)),
                  pl.BlockSpec(memory_space=pl.ANY)],
        out_specs=pl.BlockSpec(memory_space=pl.ANY),
        out_shape=jax.ShapeDtypeStruct((n_rows, d), h.dtype),
        scratch_shapes=[pltpu.SemaphoreType.DMA(())],
        input_output_aliases={2: 0},
        compiler_params=_params(("arbitrary",)),
        name="dispatch",
    )(dest3, h, xs0)


MOE_TILE = 512
MOE_SUB = 256
MOE_SLAB = 512


def _expert_kernel(n_slab, te_ref, ns_ref, nv_ref, x_ref, wg_ref, bg_ref, wu_ref, bu_ref, wd_ref, bd_ref, o_ref,
                   xb_ref, gate_ref, up_ref):
    w = pl.program_id(0)
    s = pl.program_id(1)
    n_sub = ns_ref[w]
    kt = MOE_SLAB

    @pl.when((s == 0) & (n_sub > 0))
    def _():
        for j in range(n_slab):
            xb_ref[j] = x_ref[:, j * kt:(j + 1) * kt].astype(BF16)
            gate_ref[j] = jnp.broadcast_to(bg_ref[0, :, j * kt:(j + 1) * kt], gate_ref.shape[1:])
            up_ref[j] = jnp.broadcast_to(bu_ref[0, :, j * kt:(j + 1) * kt], up_ref.shape[1:])

    @pl.when(s == n_slab)
    def _():
        o_ref[...] = jnp.broadcast_to(bd_ref[0], o_ref.shape)

    @pl.when(s < n_slab)
    def _():
        def sub(i, carry):
            rows = pl.ds(pl.multiple_of(i * MOE_SUB, MOE_SUB), MOE_SUB)
            xb = xb_ref[s, rows, :]
            gate = _mm(xb, wg_ref[0].astype(BF16))
            up = _mm(xb, wu_ref[0].astype(BF16))
            for j in range(n_slab):
                gate_ref[j, rows, :] += gate[:, j * kt:(j + 1) * kt]
                up_ref[j, rows, :] += up[:, j * kt:(j + 1) * kt]
            return carry

        lax.fori_loop(0, n_sub, sub, 0)

    @pl.when(s >= n_slab)
    def _():
        j = s - n_slab

        def sub(i, carry):
            rows = pl.ds(pl.multiple_of(i * MOE_SUB, MOE_SUB), MOE_SUB)
            gate = jnp.minimum(gate_ref[j, rows, :], SWIGLU_LIMIT)
            up = jnp.clip(up_ref[j, rows, :], -SWIGLU_LIMIT, SWIGLU_LIMIT)
            glu = gate * jax.nn.sigmoid(gate * SWIGLU_ALPHA)
            act = ((up + 1.0) * glu).astype(BF16)
            o_ref[rows, :] += _mm(act, wd_ref[0].astype(BF16))
            return carry

        lax.fori_loop(0, n_sub, sub, 0)


def _experts(xs, tile_expert, tile_nsub, n_valid, w_gate, b_gate, w_up, b_up, w_down, b_down):
    n_rows, d = xs.shape
    n_e, _, d_ff = w_gate.shape
    tm, kt = MOE_TILE, MOE_SLAB
    n_tiles = n_rows // tm
    n_slab = d // kt
    assert d == d_ff and d % kt == 0

    def tile(w, nv):
        return jnp.minimum(w, nv[0] - 1)

    def up_slab(w, s, ns):
        return jnp.where(ns[w] > 0, jnp.minimum(s, n_slab - 1), n_slab - 1)

    def down_slab(w, s, ns):
        return jnp.where(ns[w] > 0, jnp.maximum(s - n_slab, 0), n_slab - 1)

    grid_spec = pltpu.PrefetchScalarGridSpec(
        num_scalar_prefetch=3,
        grid=(n_tiles, 2 * n_slab),
        in_specs=[pl.BlockSpec((tm, d), lambda w, s, te, ns, nv: (tile(w, nv), 0)),
                  pl.BlockSpec((1, kt, d_ff), lambda w, s, te, ns, nv: (te[w], up_slab(w, s, ns), 0)),
                  pl.BlockSpec((1, 1, d_ff), lambda w, s, te, ns, nv: (te[w], 0, 0)),
                  pl.BlockSpec((1, kt, d_ff), lambda w, s, te, ns, nv: (te[w], up_slab(w, s, ns), 0)),
                  pl.BlockSpec((1, 1, d_ff), lambda w, s, te, ns, nv: (te[w], 0, 0)),
                  pl.BlockSpec((1, kt, d), lambda w, s, te, ns, nv: (te[w], down_slab(w, s, ns), 0)),
                  pl.BlockSpec((1, 1, d), lambda w, s, te, ns, nv: (te[w], 0, 0))],
        out_specs=pl.BlockSpec((tm, d), lambda w, s, te, ns, nv: (w, 0)),
        scratch_shapes=[pltpu.VMEM((n_slab, tm, kt), BF16), pltpu.VMEM((n_slab, tm, kt), F32),
                        pltpu.VMEM((n_slab, tm, kt), F32)],
    )
    return pl.pallas_call(
        functools.partial(_expert_kernel, n_slab),
        grid_spec=grid_spec,
        out_shape=jax.ShapeDtypeStruct((n_rows, d), F32),
        compiler_params=_params(("arbitrary", "arbitrary")),
        name="experts",
    )(tile_expert, tile_nsub, n_valid, xs, w_gate, b_gate.reshape(n_e, 1, d_ff), w_up,
      b_up.reshape(n_e, 1, d_ff), w_down, b_down.reshape(n_e, 1, d))


def _combine_kernel(tm, final, pos_ref, eo_ref, x1_ref, wt_ref, g_ref, o_ref, buf_ref, sem):
    def issue(r, carry):
        for j in range(TOP_K):
            _row_copy(eo_ref, pos_ref[0, 0, r * TOP_K + j], buf_ref.at[j], r, sem).start()
        return carry

    lax.fori_loop(0, tm, issue, 0)

    def drain(r, carry):
        for j in range(TOP_K):
            _row_copy(eo_ref, pos_ref[0, 0, r * TOP_K + j], buf_ref.at[j], r, sem).wait()
        return carry

    lax.fori_loop(0, tm, drain, 0)

    wt = wt_ref[...]
    y = x1_ref[...]
    for j in range(TOP_K):
        y = y + buf_ref[j] * wt[:, j:j + 1]
    if final:
        y = y * lax.rsqrt(jnp.mean(y * y, axis=-1, keepdims=True) + NORM_EPS) * g_ref[...]
    o_ref[...] = y


def _combine(eo, pos, x1, top_w, g_final, final, tm=128):
    t, d = x1.shape
    pos3 = pos.reshape(t // tm, 1, tm * TOP_K)
    return pl.pallas_call(
        functools.partial(_combine_kernel, tm, final),
        grid=(t // tm,),
        in_specs=[pl.BlockSpec((1, 1, tm * TOP_K), lambda i: (i, 0, 0), memory_space=pltpu.SMEM),
                  pl.BlockSpec(memory_space=pl.ANY),
                  pl.BlockSpec((tm, d), lambda i: (i, 0)),
                  pl.BlockSpec((tm, TOP_K), lambda i: (i, 0)),
                  pl.BlockSpec((1, d), lambda i: (0, 0))],
        out_specs=pl.BlockSpec((tm, d), lambda i: (i, 0)),
        out_shape=jax.ShapeDtypeStruct((t, d), F32),
        scratch_shapes=[pltpu.VMEM((TOP_K, tm, d), F32), pltpu.SemaphoreType.DMA(())],
        compiler_params=_params(("arbitrary",)),
        name="combine",
    )(pos3, eo, x1, top_w, g_final)


def _routing(logits):
    t = logits.shape[0]
    tm = MOE_TILE
    top_vals, top_idx = lax.top_k(logits[:, :N_EXPERTS], TOP_K)
    top_w = jax.nn.softmax(top_vals, axis=-1)
    e_flat = top_idx.reshape(-1).astype(jnp.int32)
    onehot = (e_flat[:, None] == jnp.arange(N_EXPERTS, dtype=jnp.int32)[None, :]).astype(jnp.int32)
    incl = jnp.cumsum(onehot, axis=0)
    counts = incl[-1]
    rank = jnp.sum((incl - onehot) * onehot, axis=1)
    tiles = (counts + tm - 1) // tm
    tile_end = jnp.cumsum(tiles)
    tile_start = tile_end - tiles
    dest = jnp.sum(onehot * (tile_start * tm)[None, :], axis=1) + rank
    n_tiles = (t * TOP_K) // tm + N_EXPERTS
    wid = jnp.arange(n_tiles, dtype=jnp.int32)
    n_valid = tile_end[-1:].astype(jnp.int32)
    tile_expert = jnp.minimum(jnp.searchsorted(tile_end, jnp.minimum(wid, n_valid[0] - 1), side="right"),
                              N_EXPERTS - 1).astype(jnp.int32)
    tile_rows = jnp.clip(counts[tile_expert] - (wid - tile_start[tile_expert]) * tm, 0, tm)
    tile_nsub = jnp.where(wid < n_valid[0], (tile_rows + MOE_SUB - 1) // MOE_SUB, 0).astype(jnp.int32)
    return top_w, dest.astype(jnp.int32), tile_expert, tile_nsub, n_valid, n_tiles * tm


def _pad_cols(a, width):
    return jnp.pad(a, ((0, 0), (0, width - a.shape[1])))


def _pad_rows(a, height):
    return jnp.pad(a, ((0, height - a.shape[0]), (0, 0)))


def _split_cols(a, sizes):
    out, acc = [], 0
    for s in sizes:
        out.append(a[..., acc:acc + s])
        acc += s
    return out


def kernel(x, norm_mix_g, w_in, token_shift_mu, rwkv_w0, rwkv_w2, rwkv_a0, rwkv_a2, rwkv_g2, rwkv_k_k, rwkv_k_a, rwkv_r_k, rwkv_lnx_g, rwkv_lnx_b, gla_w_alpha2, gla_b_alpha, gla_norm_g, b_merge, w_branch_rwkv, w_branch_gla, w_out, norm_ffn_g, w_router, b_router, w_gate, b_gate, w_up, b_up, w_down, b_down, norm_final_g):
    batch, seq, d = x.shape
    t = batch * seq
    depth = w_in.shape[0]
    w = RWKV_WIDTH
    sizes = (w, w, w, DECAY_LORA, ICL_LORA, GATE_LORA, GLA_KEY_WIDTH, GLA_KEY_WIDTH, GLA_VAL_WIDTH,
             GLA_VAL_WIDTH, GLA_GATE_LORA, 2 * d)

    x2d = x.reshape(t, d)
    for l in range(depth):
        s_r, s_k, s_v, s_wl, s_al, s_gl, s_gq, s_gk, s_gv, s_gog, s_gal, s_gate = _split_cols(w_in[l], sizes)
        w_p = jnp.concatenate(
            [s_gate, s_r, s_k, s_v, _pad_cols(s_wl, LANE), _pad_cols(s_al, LANE), s_gl, s_gq, s_gk, s_gv,
             s_gog, _pad_cols(s_gal, LANE)], axis=1).astype(BF16)
        mu = token_shift_mu[l][None, :]
        m_r, m_k, m_v, m_wl, m_al, m_gl = _split_cols(mu, sizes[:6])
        mus = (m_r, m_k, m_v, jnp.concatenate([_pad_cols(m_wl, LANE), _pad_cols(m_al, LANE)], axis=1), m_gl)

        p = _inproj(x2d, norm_mix_g[l][None, :], w_p)

        ya = _rwkv(p, batch, seq, mus, rwkv_w0[l][None, :], _pad_rows(rwkv_w2[l], LANE), rwkv_a0[l][None, :],
                   _pad_rows(rwkv_a2[l], LANE), rwkv_g2[l], rwkv_k_k[l][None, :], rwkv_k_a[l][None, :],
                   rwkv_r_k[l].reshape(1, w), rwkv_lnx_g[l][None, :], rwkv_lnx_b[l][None, :])

        yb = _gla(p, _pad_rows(gla_w_alpha2[l], LANE), gla_b_alpha[l][None, :], gla_norm_g[l][None, :],
                  batch, seq)

        merged = _merge(ya, yb, p, b_merge[l][None, :], w_branch_rwkv[l].astype(BF16),
                        w_branch_gla[l].astype(BF16))
        b_router_p = jnp.concatenate([b_router[l], jnp.full((LANE - N_EXPERTS,), -1e30, F32)])[None, :]
        x1, h2, logits = _outproj(merged, x2d, w_out[l].astype(BF16), norm_ffn_g[l][None, :],
                                  _pad_cols(w_router[l], LANE), b_router_p)

        top_w, dest, tile_expert, tile_nsub, n_valid, n_rows = _routing(logits)
        xs = _dispatch(h2, dest, n_rows)
        eo = _experts(xs, tile_expert, tile_nsub, n_valid, w_gate[l], b_gate[l], w_up[l], b_up[l],
                      w_down[l], b_down[l])
        final = l == depth - 1
        x2d = _combine(eo, dest, x1, top_w, norm_final_g[None, :], final)
    return x2d.reshape(batch, seq, d)
```

```python
import functools

import jax
import jax.numpy as jnp
from jax import lax
from jax.experimental import pallas as pl
from jax.experimental.pallas import tpu as pltpu

F32 = jnp.float32
BF16 = jnp.bfloat16
HI = lax.Precision.HIGHEST

NORM_EPS = 1e-5
RWKV_GN_EPS = 64e-5
RWKV_WIDTH = 1024
RWKV_HEAD = 64
DECAY_LORA = 96
ICL_LORA = 96
GATE_LORA = 256
GLA_HEADS = 4
GLA_DK = 128
GLA_DV = 256
GLA_KEY_WIDTH = GLA_HEADS * GLA_DK
GLA_VAL_WIDTH = GLA_HEADS * GLA_DV
GLA_GATE_LORA = 16
GLA_TAU = 16.0
N_EXPERTS = 32
TOP_K = 4
SWIGLU_LIMIT = 7.0
SWIGLU_ALPHA = 1.702

CHUNK = 64
SUB = 16
LANE = 128

COL_GATE = 0
COL_R = 4096
COL_K = 5120
COL_V = 6144
COL_WA = 7168
COL_GL = 7424
COL_GQ = 7680
COL_GK = 8192
COL_GV = 8704
COL_GOG = 9728
COL_GAL = 10752
N_PROJ = 10880

VMEM_LIMIT = 56 * 1024 * 1024

_DIMS = {"nn": (((1,), (0,)), ((), ())), "nt": (((1,), (1,)), ((), ())), "tn": (((0,), (0,)), ((), ()))}


def _mm(a, b, prec=None):
    return jnp.dot(a, b, preferred_element_type=F32, precision=prec)


def _bdot(kind, a, b):
    return lax.dot_general(a.astype(BF16), b.astype(BF16), _DIMS[kind], preferred_element_type=F32)


def _split_bf16(a):
    hi = a.astype(BF16)
    return hi, (a - hi.astype(F32)).astype(BF16)


def _dot3(kind, a, b):
    ah, al = _split_bf16(a)
    bh, bl = _split_bf16(b)
    return _bdot(kind, ah, bh) + _bdot(kind, ah, bl) + _bdot(kind, al, bh)


def _seg_sum(x, e_ref):
    hi, lo = _split_bf16(x)
    e = e_ref[...]
    return _mm(hi, e) + _mm(lo, e)


def _params(sem, vmem=VMEM_LIMIT):
    return pltpu.CompilerParams(dimension_semantics=sem, vmem_limit_bytes=vmem)


N_STREAMS = 4


def _inproj_kernel(x_ref, g_ref, *rest):
    w_refs, (o_ref, h_ref) = rest[:N_STREAMS], rest[N_STREAMS:]

    @pl.when(pl.program_id(1) == 0)
    def _():
        x = x_ref[...]
        ms = jnp.mean(x * x, axis=-1, keepdims=True)
        h_ref[...] = (x * lax.rsqrt(ms + NORM_EPS) * g_ref[...]).astype(BF16)

    o_ref[...] = _mm(h_ref[...], jnp.concatenate([r[...] for r in w_refs], axis=0))


def _inproj(x2d, g, w_p, tm=512, tn=640):
    t, d = x2d.shape
    n = w_p.shape[1]
    piece = d // N_STREAMS
    w_specs = [pl.BlockSpec((piece, tn), functools.partial(lambda q, i, j: (q, j), q)) for q in range(N_STREAMS)]
    return pl.pallas_call(
        _inproj_kernel,
        grid=(t // tm, n // tn),
        in_specs=[pl.BlockSpec((tm, d), lambda i, j: (i, 0)),
                  pl.BlockSpec((1, d), lambda i, j: (0, 0))] + w_specs,
        out_specs=pl.BlockSpec((tm, tn), lambda i, j: (i, j)),
        out_shape=jax.ShapeDtypeStruct((t, n), F32),
        scratch_shapes=[pltpu.VMEM((tm, d), BF16)],
        compiler_params=_params(("parallel", "arbitrary")),
        name="inproj",
    )(x2d, g, *([w_p] * N_STREAMS))


def _softplus(y):
    return jnp.maximum(y, 0.0) + jnp.log(1.0 + jnp.exp(-jnp.abs(y)))


def _rwkv_kernel(tb, r_ref, k_ref, v_ref, wa_ref, gl_ref,
                 mur_ref, muk_ref, muv_ref, muwa_ref, mugl_ref,
                 w0_ref, w2_ref, a0_ref, a2_ref, g2_ref, kk_ref, ka_ref, rk_ref, lg_ref, lb_ref,
                 tri_ref, e_ref, o_ref,
                 s_ref, pr_ref, pk_ref, pv_ref, pwa_ref, pgl_ref,
                 ah_ref, al_ref, bh_ref, bl_ref, kh_ref, kl_ref, rh_ref, rl_ref, vb_ref,
                 cum_ref, y_ref, bonus_ref, g_ref):
    @pl.when(pl.program_id(2) == 0)
    def _():
        s_ref[...] = jnp.zeros_like(s_ref)
        for ref in (pr_ref, pk_ref, pv_ref, pwa_ref, pgl_ref):
            ref[...] = jnp.zeros_like(ref)

    def shift_mix(cur_ref, prev_ref, mu_ref):
        cur = cur_ref[...]
        row = lax.broadcasted_iota(jnp.int32, cur.shape, 0)
        shifted = jnp.where(row == 0, prev_ref[0:1, :], pltpu.roll(cur, 1, axis=0))
        prev_ref[0:1, :] = cur[tb - 1:tb, :]
        return cur + (shifted - cur) * mu_ref[...]

    r = shift_mix(r_ref, pr_ref, mur_ref)
    k = shift_mix(k_ref, pk_ref, muk_ref)
    v = shift_mix(v_ref, pv_ref, muv_ref)
    wa = shift_mix(wa_ref, pwa_ref, muwa_ref)
    gl = shift_mix(gl_ref, pgl_ref, mugl_ref)

    z = w0_ref[...] + _dot3("nn", jnp.tanh(wa[:, :LANE]), w2_ref[...])
    logw = -jnp.exp(-_softplus(-z) - 0.5)
    a = jax.nn.sigmoid(a0_ref[...] + _dot3("nn", wa[:, LANE:], a2_ref[...]))
    g_ref[...] = _bdot("nn", jax.nn.sigmoid(gl), g2_ref[...])

    kk = k * kk_ref[...]
    kk = kk / jnp.maximum(jnp.sqrt(_seg_sum(kk * kk, e_ref)), 1e-12)
    k2 = k * (1.0 + (a - 1.0) * ka_ref[...])
    bonus_ref[...] = _seg_sum(r * k2 * rk_ref[...], e_ref) * v

    lw_hi, lw_lo = _split_bf16(logw)
    cum = _mm(tri_ref[...], lw_hi) + _mm(tri_ref[...], lw_lo)
    inv = jnp.exp(-cum)
    cum_ref[...] = cum
    for val, hi_ref, lo_ref in ((-kk * jnp.exp(cum - logw), ah_ref, al_ref), (kk * a * inv, bh_ref, bl_ref),
                                (k2 * inv, kh_ref, kl_ref), (r * jnp.exp(cum), rh_ref, rl_ref)):
        hi, lo = _split_bf16(val)
        hi_ref[...] = hi
        lo_ref[...] = lo
    vb_ref[...] = v.astype(BF16)

    c2 = 2 * CHUNK
    lane = lax.broadcasted_iota(jnp.int32, (CHUNK, LANE), 1)
    head0 = lane < RWKV_HEAD
    rows = lax.broadcasted_iota(jnp.int32, (c2, c2), 0)
    cols = lax.broadcasted_iota(jnp.int32, (c2, c2), 1)
    strict = rows > cols
    incl = rows >= cols
    eye = (rows == cols).astype(F32)
    zero = jnp.zeros((CHUNK, LANE), BF16)

    def stack(ref, c):
        x = ref[c * CHUNK:(c + 1) * CHUNK, :]
        return jnp.concatenate([jnp.where(head0, x, zero), jnp.where(head0, zero, x)], axis=0)

    chunks = range(tb // CHUNK)
    a_h, b_h, k_h, r_h, v_s = ([stack(ref, c) for c in chunks] for ref in (ah_ref, bh_ref, kh_ref, rh_ref, vb_ref))
    a_l, b_l, k_l, r_l = ([stack(ref, c) for c in chunks] for ref in (al_ref, bl_ref, kl_ref, rl_ref))
    ar_h = [jnp.concatenate([a_h[c], r_h[c]], axis=0) for c in chunks]
    bk_h = [jnp.concatenate([b_h[c], k_h[c]], axis=0) for c in chunks]
    gram = [_bdot("nt", ar_h[c], bk_h[c]) + _bdot("nt", ar_h[c], jnp.concatenate([b_l[c], k_l[c]], axis=0))
            + _bdot("nt", jnp.concatenate([a_l[c], r_l[c]], axis=0), bk_h[c]) for c in chunks]
    l_ba = [jnp.where(strict, gram[c][:c2, :c2], 0.0) for c in chunks]
    l_ka = [jnp.where(strict, gram[c][:c2, c2:], 0.0).astype(BF16) for c in chunks]
    g_b = [jnp.where(incl, gram[c][c2:, :c2], 0.0).astype(BF16) for c in chunks]
    g_k = [jnp.where(incl, gram[c][c2:, c2:], 0.0).astype(BF16) for c in chunks]

    t_inv = [eye + l_ba[c] for c in chunks]
    x = [l_ba[c].astype(BF16) for c in chunks]
    step = 2
    while step < CHUNK:
        x = [_bdot("nn", x[c], x[c]).astype(BF16) for c in chunks]
        t_inv = [t_inv[c] + _bdot("nn", t_inv[c], x[c]) for c in chunks]
        step *= 2

    w2 = [_bdot("nn", l_ka[c], v_s[c]).astype(BF16) for c in chunks]
    p12 = [_bdot("nn", t_inv[c], jnp.concatenate([a_h[c], w2[c]], axis=1)).astype(BF16) for c in chunks]
    qz = [_bdot("nn", g_b[c], p12[c]) for c in chunks]
    q = [(r_h[c].astype(F32) + r_l[c].astype(F32) + qz[c][:, :LANE]).astype(BF16) for c in chunks]
    z_c = [qz[c][:, LANE:] + _bdot("nn", g_k[c], v_s[c]) for c in chunks]
    mn = [_bdot("tn", p12[c], b_h[c]) for c in chunks]
    vk = [_bdot("tn", v_s[c], k_h[c]) for c in chunks]

    s = s_ref[...]
    for c in chunks:
        g_end = jnp.exp(cum_ref[(c + 1) * CHUNK - 1:(c + 1) * CHUNK, :])
        y_st = _bdot("nt", q[c], s) + z_c[c]
        y_ref[c * CHUNK:(c + 1) * CHUNK, :] = y_st[:CHUNK] + y_st[CHUNK:]
        m = (eye + mn[c][:LANE]) * g_end
        n = (mn[c][LANE:] + vk[c]) * g_end
        s = _bdot("nn", s, m) + n
    s_ref[...] = s

    y = y_ref[...]
    inv_n = 1.0 / RWKV_HEAD
    mean = _seg_sum(y, e_ref) * inv_n
    d = y - mean
    var = _seg_sum(d * d, e_ref) * inv_n
    yn = d * lax.rsqrt(var + RWKV_GN_EPS) * lg_ref[...] + lb_ref[...]
    o_ref[...] = ((yn + bonus_ref[...]) * g_ref[...]).astype(o_ref.dtype)


def _rwkv(p, batch, seq, mus, w0, w2p, a0, a2p, g2, k_k, k_a, r_k, lnx_g, lnx_b, tb=512):
    t = p.shape[0]
    w = RWKV_WIDTH
    blocks_per_seq = seq // tb

    def seg(width, base, per_head):
        return pl.BlockSpec((tb, width),
                            lambda b, h, i: (b * blocks_per_seq + i, base // width + (h if per_head else 0)))

    def vec(width, per_head=True):
        return pl.BlockSpec((1, width), lambda b, h, i: (0, h if per_head else 0))

    def mat(rows_):
        return pl.BlockSpec((rows_, LANE), lambda b, h, i: (0, h))

    rows = lax.broadcasted_iota(jnp.int32, (tb, tb), 0)
    cols = lax.broadcasted_iota(jnp.int32, (tb, tb), 1)
    tri = ((rows // CHUNK == cols // CHUNK) & (rows >= cols)).astype(BF16)
    rows = lax.broadcasted_iota(jnp.int32, (LANE, LANE), 0)
    cols = lax.broadcasted_iota(jnp.int32, (LANE, LANE), 1)
    e_seg = (rows // RWKV_HEAD == cols // RWKV_HEAD).astype(BF16)

    in_specs = [seg(LANE, COL_R, True), seg(LANE, COL_K, True), seg(LANE, COL_V, True),
                seg(2 * LANE, COL_WA, False), seg(GATE_LORA, COL_GL, False),
                vec(LANE), vec(LANE), vec(LANE), vec(2 * LANE, False), vec(GATE_LORA, False),
                vec(LANE), mat(LANE), vec(LANE), mat(LANE), mat(GATE_LORA), vec(LANE), vec(LANE), vec(LANE),
                vec(LANE), vec(LANE),
                pl.BlockSpec((tb, tb), lambda b, h, i: (0, 0)),
                pl.BlockSpec((LANE, LANE), lambda b, h, i: (0, 0))]
    half = pltpu.VMEM((tb, LANE), BF16)
    full = pltpu.VMEM((tb, LANE), F32)
    return pl.pallas_call(
        functools.partial(_rwkv_kernel, tb),
        grid=(batch, w // LANE, blocks_per_seq),
        in_specs=in_specs,
        out_specs=pl.BlockSpec((tb, LANE), lambda b, h, i: (b * blocks_per_seq + i, h)),
        out_shape=jax.ShapeDtypeStruct((t, w), BF16),
        scratch_shapes=[pltpu.VMEM((LANE, LANE), F32),
                        pltpu.VMEM((8, LANE), F32), pltpu.VMEM((8, LANE), F32), pltpu.VMEM((8, LANE), F32),
                        pltpu.VMEM((8, 2 * LANE), F32), pltpu.VMEM((8, GATE_LORA), F32)]
                       + [half] * 9 + [full] * 4,
        compiler_params=_params(("parallel", "parallel", "arbitrary")),
        name="rwkv",
    )(p, p, p, p, p, *mus, w0, w2p, a0, a2p, g2, k_k, k_a, r_k, lnx_g, lnx_b, tri, e_seg)


def _gla_kernel(n_chunks, q_ref, k_ref, v_ref, og_ref, al_ref, wal_ref, bal_ref, ng_ref, o_ref, s_ref):
    @pl.when(pl.program_id(2) == 0)
    def _():
        s_ref[...] = jnp.zeros_like(s_ref)

    n_sub = CHUNK // SUB
    scale = GLA_DK ** -0.5
    rows = lax.broadcasted_iota(jnp.int32, (CHUNK, CHUNK), 0)
    cols = lax.broadcasted_iota(jnp.int32, (CHUNK, CHUNK), 1)
    tri = (rows >= cols).astype(BF16)
    srow = lax.broadcasted_iota(jnp.int32, (CHUNK, LANE), 0)
    slane = lax.broadcasted_iota(jnp.int32, (CHUNK, LANE), 1)

    def body(c, carry):
        sl = pl.ds(pl.multiple_of(c * CHUNK, CHUNK), CHUNK)
        q = q_ref[sl, :] * scale
        k = k_ref[sl, :]
        v = v_ref[sl, :]
        pre = _dot3("nn", al_ref[sl, :], wal_ref[...]) + bal_ref[...]
        g = jax.nn.log_sigmoid(pre) / GLA_TAU
        g_hi, g_lo = _split_bf16(g)
        b = _mm(tri, g_hi) + _mm(tri, g_lo)

        pieces = []
        for i in range(n_sub):
            lo, hi = i * SUB, (i + 1) * SUB
            base = b[lo - 1:lo, :] if i > 0 else jnp.zeros((1, LANE), F32)
            q_i = q[lo:hi, :] * jnp.exp(b[lo:hi, :] - base)
            if i > 0:
                k_hat = k[:lo, :] * jnp.exp(base - b[:lo, :])
                top = _bdot("nt", k_hat, q_i)
                pieces.append(jnp.concatenate([top, jnp.zeros((CHUNK - lo, SUB), F32)], axis=0))
            else:
                pieces.append(jnp.zeros((CHUNK, SUB), F32))
        pieces.append(jnp.zeros((CHUNK, LANE - CHUNK), F32))
        a_t = jnp.concatenate(pieces, axis=1)

        for tau in range(SUB):
            q_b = jnp.concatenate([jnp.broadcast_to(q[i * SUB + tau:i * SUB + tau + 1, :], (SUB, LANE))
                                   for i in range(n_sub)], axis=0)
            b_b = jnp.concatenate([jnp.broadcast_to(b[i * SUB + tau:i * SUB + tau + 1, :], (SUB, LANE))
                                   for i in range(n_sub)], axis=0)
            ok = (srow % SUB) <= tau
            dec = jnp.where(ok, jnp.exp(jnp.where(ok, b_b - b, 0.0)), 0.0)
            col = jnp.sum(k * q_b * dec, axis=-1, keepdims=True)
            hit = (slane % SUB == tau) & (slane // SUB == srow // SUB)
            a_t = a_t + jnp.where(hit, col, 0.0)

        s = s_ref[...]
        o = _bdot("tn", a_t[:, :CHUNK], v) + _bdot("nt", q * jnp.exp(b), s)
        b_last = b[CHUNK - 1:CHUNK, :]
        s_ref[...] = s * jnp.exp(b_last) + _bdot("tn", v, k * jnp.exp(b_last - b))

        o = o * lax.rsqrt(jnp.mean(o * o, axis=-1, keepdims=True) + NORM_EPS) * ng_ref[...]
        og = og_ref[sl, :]
        o_ref[sl, :] = (o * (og * jax.nn.sigmoid(og))).astype(o_ref.dtype)
        return carry

    lax.fori_loop(0, n_chunks, body, 0)


def _gla(p, w_alpha2p, b_alpha, norm_g, batch, seq, tb=256):
    t = p.shape[0]
    blocks_per_seq = seq // tb

    def col(width, base):
        return pl.BlockSpec((tb, width), lambda b, h, i: (b * blocks_per_seq + i, base // width + h))

    return pl.pallas_call(
        functools.partial(_gla_kernel, tb // CHUNK),
        grid=(batch, GLA_HEADS, blocks_per_seq),
        in_specs=[col(GLA_DK, COL_GQ), col(GLA_DK, COL_GK), col(GLA_DV, COL_GV), col(GLA_DV, COL_GOG),
                  pl.BlockSpec((tb, LANE), lambda b, h, i: (b * blocks_per_seq + i, COL_GAL // LANE)),
                  pl.BlockSpec((LANE, GLA_DK), lambda b, h, i: (0, h)),
                  pl.BlockSpec((1, GLA_DK), lambda b, h, i: (0, h)),
                  pl.BlockSpec((1, GLA_DV), lambda b, h, i: (0, 0))],
        out_specs=pl.BlockSpec((tb, GLA_DV), lambda b, h, i: (b * blocks_per_seq + i, h)),
        out_shape=jax.ShapeDtypeStruct((t, GLA_VAL_WIDTH), BF16),
        scratch_shapes=[pltpu.VMEM((GLA_DV, GLA_DK), F32)],
        compiler_params=_params(("parallel", "parallel", "arbitrary")),
        name="gla",
    )(p, p, p, p, p, w_alpha2p, b_alpha, norm_g)


def _merge_kernel(ya_ref, yb_ref, pg_ref, bm_ref, wa_ref, wb_ref, o_ref):
    d = wa_ref.shape[1]
    gates = jax.nn.sigmoid(pg_ref[...] + bm_ref[...])
    ma = _mm(ya_ref[...], wa_ref[...])
    mb = _mm(yb_ref[...], wb_ref[...])
    o_ref[...] = (gates[:, :d] * ma + gates[:, d:] * mb).astype(o_ref.dtype)


def _merge(ya, yb, p, b_merge, wa, wb, tm=256):
    t = ya.shape[0]
    d = wa.shape[1]
    return pl.pallas_call(
        _merge_kernel,
        grid=(t // tm,),
        in_specs=[pl.BlockSpec((tm, ya.shape[1]), lambda i: (i, 0)),
                  pl.BlockSpec((tm, yb.shape[1]), lambda i: (i, 0)),
                  pl.BlockSpec((tm, 2 * d), lambda i: (i, COL_GATE // (2 * d))),
                  pl.BlockSpec((1, 2 * d), lambda i: (0, 0)),
                  pl.BlockSpec(wa.shape, lambda i: (0, 0)),
                  pl.BlockSpec(wb.shape, lambda i: (0, 0))],
        out_specs=pl.BlockSpec((tm, d), lambda i: (i, 0)),
        out_shape=jax.ShapeDtypeStruct((t, d), BF16),
        compiler_params=_params(("parallel",)),
        name="merge",
    )(ya, yb, p, b_merge, wa, wb)


def _outproj_kernel(m_ref, x_ref, wo_ref, g_ref, wr_ref, br_ref, x1_ref, h_ref, lg_ref):
    x1 = x_ref[...] + _mm(m_ref[...], wo_ref[...])
    x1_ref[...] = x1
    ms = jnp.mean(x1 * x1, axis=-1, keepdims=True)
    h = x1 * lax.rsqrt(ms + NORM_EPS) * g_ref[...]
    h_ref[...] = h
    lg_ref[...] = _mm(h, wr_ref[...], HI) + br_ref[...]


def _outproj(merged, x2d, w_out, g_ffn, w_router_p, b_router_p, tm=256):
    t, d = x2d.shape
    row = pl.BlockSpec((tm, d), lambda i: (i, 0))
    return pl.pallas_call(
        _outproj_kernel,
        grid=(t // tm,),
        in_specs=[row, row,
                  pl.BlockSpec((d, d), lambda i: (0, 0)),
                  pl.BlockSpec((1, d), lambda i: (0, 0)),
                  pl.BlockSpec((d, LANE), lambda i: (0, 0)),
                  pl.BlockSpec((1, LANE), lambda i: (0, 0))],
        out_specs=[row, row, pl.BlockSpec((tm, LANE), lambda i: (i, 0))],
        out_shape=[jax.ShapeDtypeStruct((t, d), F32), jax.ShapeDtypeStruct((t, d), F32),
                   jax.ShapeDtypeStruct((t, LANE), F32)],
        compiler_params=_params(("parallel",)),
        name="outproj",
    )(merged, x2d, w_out, g_ffn, w_router_p, b_router_p)


def _row_copy(src_ref, src_row, dst_ref, dst_row, sem):
    return pltpu.make_async_copy(src_ref.at[pl.ds(src_row, 1)], dst_ref.at[pl.ds(dst_row, 1)], sem)


def _dispatch_kernel(tm, dest_ref, h_ref, xs_in_ref, xs_ref, sem):
    del xs_in_ref

    def issue(r, carry):
        for j in range(TOP_K):
            _row_copy(h_ref, r, xs_ref, dest_ref[0, 0, r * TOP_K + j], sem).start()
        return carry

    lax.fori_loop(0, tm, issue, 0)

    def drain(r, carry):
        for j in range(TOP_K):
            _row_copy(h_ref, r, xs_ref, dest_ref[0, 0, r * TOP_K + j], sem).wait()
        return carry

    lax.fori_loop(0, tm, drain, 0)


def _dispatch(h, dest, n_rows, tm=256):
    t, d = h.shape
    xs0 = jnp.zeros((n_rows, d), h.dtype)
    dest3 = dest.reshape(t // tm, 1, tm * TOP_K)
    return pl.pallas_call(
        functools.partial(_dispatch_kernel, tm),
        grid=(t // tm,),
        in_specs=[pl.BlockSpec((1, 1, tm * TOP_K), lambda i: (i, 0, 0), memory_space=pltpu.SMEM),
                  pl.BlockSpec((tm, d), lambda i: (i, 0)),
                  pl.BlockSpec(memory_space=pl.ANY)],
        out_specs=pl.BlockSpec(memory_space=pl.ANY),
        out_shape=jax.ShapeDtypeStruct((n_rows, d), h.dtype),
        scratch_shapes=[pltpu.SemaphoreType.DMA(())],
        input_output_aliases={2: 0},
        compiler_params=_params(("arbitrary",)),
        name="dispatch",
    )(dest3, h, xs0)


MOE_TILE = 512
MOE_SUB = 256
MOE_SLAB = 512


def _expert_kernel(n_slab, te_ref, ns_ref, nv_ref, x_ref, bg_ref, bu_ref, bd_ref, *rest):
    wg_refs, wu_refs, wd_refs = (rest[q * N_STREAMS:(q + 1) * N_STREAMS] for q in range(3))
    o_ref, xb_ref, gate_ref, up_ref = rest[3 * N_STREAMS:]

    def slab(refs):
        return jnp.concatenate([r[0].astype(BF16) for r in refs], axis=0)

    w = pl.program_id(0)
    s = pl.program_id(1)
    n_sub = ns_ref[w]
    kt = MOE_SLAB

    @pl.when((s == 0) & (n_sub > 0))
    def _():
        for j in range(n_slab):
            xb_ref[j] = x_ref[:, j * kt:(j + 1) * kt].astype(BF16)
            gate_ref[j] = jnp.broadcast_to(bg_ref[0, :, j * kt:(j + 1) * kt], gate_ref.shape[1:])
            up_ref[j] = jnp.broadcast_to(bu_ref[0, :, j * kt:(j + 1) * kt], up_ref.shape[1:])

    @pl.when(s == n_slab)
    def _():
        o_ref[...] = jnp.broadcast_to(bd_ref[0], o_ref.shape)

    @pl.when(s < n_slab)
    def _():
        def sub(i, carry):
            rows = pl.ds(pl.multiple_of(i * MOE_SUB, MOE_SUB), MOE_SUB)
            xb = xb_ref[s, rows, :]
            gate = _mm(xb, slab(wg_refs))
            up = _mm(xb, slab(wu_refs))
            for j in range(n_slab):
                gate_ref[j, rows, :] += gate[:, j * kt:(j + 1) * kt]
                up_ref[j, rows, :] += up[:, j * kt:(j + 1) * kt]
            return carry

        lax.fori_loop(0, n_sub, sub, 0)

    @pl.when(s >= n_slab)
    def _():
        j = s - n_slab

        def sub(i, carry):
            rows = pl.ds(pl.multiple_of(i * MOE_SUB, MOE_SUB), MOE_SUB)
            gate = jnp.minimum(gate_ref[j, rows, :], SWIGLU_LIMIT)
            up = jnp.clip(up_ref[j, rows, :], -SWIGLU_LIMIT, SWIGLU_LIMIT)
            glu = gate * jax.nn.sigmoid(gate * SWIGLU_ALPHA)
            act = ((up + 1.0) * glu).astype(BF16)
            o_ref[rows, :] += _mm(act, slab(wd_refs))
            return carry

        lax.fori_loop(0, n_sub, sub, 0)


def _experts(xs, tile_expert, tile_nsub, n_valid, w_gate, b_gate, w_up, b_up, w_down, b_down):
    n_rows, d = xs.shape
    n_e, _, d_ff = w_gate.shape
    tm, kt = MOE_TILE, MOE_SLAB
    n_tiles = n_rows // tm
    n_slab = d // kt
    assert d == d_ff and d % kt == 0

    def tile(w, nv):
        return jnp.minimum(w, nv[0] - 1)

    def up_slab(w, s, ns):
        return jnp.where(ns[w] > 0, jnp.minimum(s, n_slab - 1), n_slab - 1)

    def down_slab(w, s, ns):
        return jnp.where(ns[w] > 0, jnp.maximum(s - n_slab, 0), n_slab - 1)

    def piece_spec(slab_of, q):
        return pl.BlockSpec((1, kt // N_STREAMS, d),
                            lambda w, s, te, ns, nv: (te[w], slab_of(w, s, ns) * N_STREAMS + q, 0))

    grid_spec = pltpu.PrefetchScalarGridSpec(
        num_scalar_prefetch=3,
        grid=(n_tiles, 2 * n_slab),
        in_specs=[pl.BlockSpec((tm, d), lambda w, s, te, ns, nv: (tile(w, nv), 0)),
                  pl.BlockSpec((1, 1, d_ff), lambda w, s, te, ns, nv: (te[w], 0, 0)),
                  pl.BlockSpec((1, 1, d_ff), lambda w, s, te, ns, nv: (te[w], 0, 0)),
                  pl.BlockSpec((1, 1, d), lambda w, s, te, ns, nv: (te[w], 0, 0))]
                 + [piece_spec(up_slab, q) for q in range(N_STREAMS)] * 2
                 + [piece_spec(down_slab, q) for q in range(N_STREAMS)],
        out_specs=pl.BlockSpec((tm, d), lambda w, s, te, ns, nv: (w, 0)),
        scratch_shapes=[pltpu.VMEM((n_slab, tm, kt), BF16), pltpu.VMEM((n_slab, tm, kt), F32),
                        pltpu.VMEM((n_slab, tm, kt), F32)],
    )
    return pl.pallas_call(
        functools.partial(_expert_kernel, n_slab),
        grid_spec=grid_spec,
        out_shape=jax.ShapeDtypeStruct((n_rows, d), F32),
        compiler_params=_params(("arbitrary", "arbitrary")),
        name="experts",
    )(tile_expert, tile_nsub, n_valid, xs, b_gate.reshape(n_e, 1, d_ff), b_up.reshape(n_e, 1, d_ff),
      b_down.reshape(n_e, 1, d), *([w_gate] * N_STREAMS), *([w_up] * N_STREAMS), *([w_down] * N_STREAMS))


def _combine_kernel(tm, final, pos_ref, eo_ref, x1_ref, wt_ref, g_ref, o_ref, buf_ref, sem):
    def issue(r, carry):
        for j in range(TOP_K):
            _row_copy(eo_ref, pos_ref[0, 0, r * TOP_K + j], buf_ref.at[j], r, sem).start()
        return carry

    lax.fori_loop(0, tm, issue, 0)

    def drain(r, carry):
        for j in range(TOP_K):
            _row_copy(eo_ref, pos_ref[0, 0, r * TOP_K + j], buf_ref.at[j], r, sem).wait()
        return carry

    lax.fori_loop(0, tm, drain, 0)

    wt = wt_ref[...]
    y = x1_ref[...]
    for j in range(TOP_K):
        y = y + buf_ref[j] * wt[:, j:j + 1]
    if final:
        y = y * lax.rsqrt(jnp.mean(y * y, axis=-1, keepdims=True) + NORM_EPS) * g_ref[...]
    o_ref[...] = y


def _combine(eo, pos, x1, top_w, g_final, final, tm=128):
    t, d = x1.shape
    pos3 = pos.reshape(t // tm, 1, tm * TOP_K)
    return pl.pallas_call(
        functools.partial(_combine_kernel, tm, final),
        grid=(t // tm,),
        in_specs=[pl.BlockSpec((1, 1, tm * TOP_K), lambda i: (i, 0, 0), memory_space=pltpu.SMEM),
                  pl.BlockSpec(memory_space=pl.ANY),
                  pl.BlockSpec((tm, d), lambda i: (i, 0)),
                  pl.BlockSpec((tm, TOP_K), lambda i: (i, 0)),
                  pl.BlockSpec((1, d), lambda i: (0, 0))],
        out_specs=pl.BlockSpec((tm, d), lambda i: (i, 0)),
        out_shape=jax.ShapeDtypeStruct((t, d), F32),
        scratch_shapes=[pltpu.VMEM((TOP_K, tm, d), F32), pltpu.SemaphoreType.DMA(())],
        compiler_params=_params(("arbitrary",)),
        name="combine",
    )(pos3, eo, x1, top_w, g_final)


def _routing(logits):
    t = logits.shape[0]
    tm = MOE_TILE
    top_vals, top_idx = lax.top_k(logits[:, :N_EXPERTS], TOP_K)
    top_w = jax.nn.softmax(top_vals, axis=-1)
    e_flat = top_idx.reshape(-1).astype(jnp.int32)
    onehot = (e_flat[:, None] == jnp.arange(N_EXPERTS, dtype=jnp.int32)[None, :]).astype(jnp.int32)
    incl = jnp.cumsum(onehot, axis=0)
    counts = incl[-1]
    rank = jnp.sum((incl - onehot) * onehot, axis=1)
    tiles = (counts + tm - 1) // tm
    tile_end = jnp.cumsum(tiles)
    tile_start = tile_end - tiles
    dest = jnp.sum(onehot * (tile_start * tm)[None, :], axis=1) + rank
    n_tiles = (t * TOP_K) // tm + N_EXPERTS
    wid = jnp.arange(n_tiles, dtype=jnp.int32)
    n_valid = tile_end[-1:].astype(jnp.int32)
    tile_expert = jnp.minimum(jnp.searchsorted(tile_end, jnp.minimum(wid, n_valid[0] - 1), side="right"),
                              N_EXPERTS - 1).astype(jnp.int32)
    tile_rows = jnp.clip(counts[tile_expert] - (wid - tile_start[tile_expert]) * tm, 0, tm)
    tile_nsub = jnp.where(wid < n_valid[0], (tile_rows + MOE_SUB - 1) // MOE_SUB, 0).astype(jnp.int32)
    return top_w, dest.astype(jnp.int32), tile_expert, tile_nsub, n_valid, n_tiles * tm


def _pad_cols(a, width):
    return jnp.pad(a, ((0, 0), (0, width - a.shape[1])))


def _pad_rows(a, height):
    return jnp.pad(a, ((0, height - a.shape[0]), (0, 0)))


def _split_cols(a, sizes):
    out, acc = [], 0
    for s in sizes:
        out.append(a[..., acc:acc + s])
        acc += s
    return out


def kernel(x, norm_mix_g, w_in, token_shift_mu, rwkv_w0, rwkv_w2, rwkv_a0, rwkv_a2, rwkv_g2, rwkv_k_k, rwkv_k_a, rwkv_r_k, rwkv_lnx_g, rwkv_lnx_b, gla_w_alpha2, gla_b_alpha, gla_norm_g, b_merge, w_branch_rwkv, w_branch_gla, w_out, norm_ffn_g, w_router, b_router, w_gate, b_gate, w_up, b_up, w_down, b_down, norm_final_g):
    batch, seq, d = x.shape
    t = batch * seq
    depth = w_in.shape[0]
    w = RWKV_WIDTH
    sizes = (w, w, w, DECAY_LORA, ICL_LORA, GATE_LORA, GLA_KEY_WIDTH, GLA_KEY_WIDTH, GLA_VAL_WIDTH,
             GLA_VAL_WIDTH, GLA_GATE_LORA, 2 * d)

    x2d = x.reshape(t, d)
    for l in range(depth):
        s_r, s_k, s_v, s_wl, s_al, s_gl, s_gq, s_gk, s_gv, s_gog, s_gal, s_gate = _split_cols(w_in[l], sizes)
        w_p = jnp.concatenate(
            [s_gate, s_r, s_k, s_v, _pad_cols(s_wl, LANE), _pad_cols(s_al, LANE), s_gl, s_gq, s_gk, s_gv,
             s_gog, _pad_cols(s_gal, LANE)], axis=1).astype(BF16)
        mu = token_shift_mu[l][None, :]
        m_r, m_k, m_v, m_wl, m_al, m_gl = _split_cols(mu, sizes[:6])
        mus = (m_r, m_k, m_v, jnp.concatenate([_pad_cols(m_wl, LANE), _pad_cols(m_al, LANE)], axis=1), m_gl)

        p = _inproj(x2d, norm_mix_g[l][None, :], w_p)

        ya = _rwkv(p, batch, seq, mus, rwkv_w0[l][None, :], _pad_rows(rwkv_w2[l], LANE), rwkv_a0[l][None, :],
                   _pad_rows(rwkv_a2[l], LANE), rwkv_g2[l], rwkv_k_k[l][None, :], rwkv_k_a[l][None, :],
                   rwkv_r_k[l].reshape(1, w), rwkv_lnx_g[l][None, :], rwkv_lnx_b[l][None, :])

        yb = _gla(p, _pad_rows(gla_w_alpha2[l], LANE), gla_b_alpha[l][None, :], gla_norm_g[l][None, :],
                  batch, seq)

        merged = _merge(ya, yb, p, b_merge[l][None, :], w_branch_rwkv[l].astype(BF16),
                        w_branch_gla[l].astype(BF16))
        b_router_p = jnp.concatenate([b_router[l], jnp.full((LANE - N_EXPERTS,), -1e30, F32)])[None, :]
        x1, h2, logits = _outproj(merged, x2d, w_out[l].astype(BF16), norm_ffn_g[l][None, :],
                                  _pad_cols(w_router[l], LANE), b_router_p)

        top_w, dest, tile_expert, tile_nsub, n_valid, n_rows = _routing(logits)
        xs = _dispatch(h2, dest, n_rows)
        eo = _experts(xs, tile_expert, tile_nsub, n_valid, w_gate[l], b_gate[l], w_up[l], b_up[l],
                      w_down[l], b_down[l])
        final = l == depth - 1
        x2d = _combine(eo, dest, x1, top_w, norm_final_g[None, :], final)
    return x2d.reshape(batch, seq, d)
```

```python
import functools

import jax
import jax.numpy as jnp
from jax import lax
from jax.experimental import pallas as pl
from jax.experimental.pallas import tpu as pltpu

F32 = jnp.float32
BF16 = jnp.bfloat16
HI = lax.Precision.HIGHEST

NORM_EPS = 1e-5
RWKV_GN_EPS = 64e-5
RWKV_WIDTH = 1024
RWKV_HEAD = 64
DECAY_LORA = 96
ICL_LORA = 96
GATE_LORA = 256
GLA_HEADS = 4
GLA_DK = 128
GLA_DV = 256
GLA_KEY_WIDTH = GLA_HEADS * GLA_DK
GLA_VAL_WIDTH = GLA_HEADS * GLA_DV
GLA_GATE_LORA = 16
GLA_TAU = 16.0
N_EXPERTS = 32
TOP_K = 4
SWIGLU_LIMIT = 7.0
SWIGLU_ALPHA = 1.702

CHUNK = 64
SUB = 16
LANE = 128

COL_GATE = 0
COL_R = 4096
COL_K = 5120
COL_V = 6144
COL_WA = 7168
COL_GL = 7424
COL_GQ = 7680
COL_GK = 8192
COL_GV = 8704
COL_GOG = 9728
COL_GAL = 10752
N_PROJ = 10880

VMEM_LIMIT = 56 * 1024 * 1024

_DIMS = {"nn": (((1,), (0,)), ((), ())), "nt": (((1,), (1,)), ((), ())), "tn": (((0,), (0,)), ((), ()))}


def _mm(a, b, prec=None):
    return jnp.dot(a, b, preferred_element_type=F32, precision=prec)


def _bdot(kind, a, b):
    return lax.dot_general(a.astype(BF16), b.astype(BF16), _DIMS[kind], preferred_element_type=F32)


def _split_bf16(a):
    hi = a.astype(BF16)
    return hi, (a - hi.astype(F32)).astype(BF16)


def _dot3(kind, a, b):
    ah, al = _split_bf16(a)
    bh, bl = _split_bf16(b)
    return _bdot(kind, ah, bh) + _bdot(kind, ah, bl) + _bdot(kind, al, bh)


def _seg_sum(x, e_ref):
    hi, lo = _split_bf16(x)
    e = e_ref[...]
    return _mm(hi, e) + _mm(lo, e)


def _params(sem, vmem=VMEM_LIMIT):
    return pltpu.CompilerParams(dimension_semantics=sem, vmem_limit_bytes=vmem)


N_STREAMS = 4


def _inproj_kernel(x_ref, g_ref, *rest):
    w_refs, (o_ref, h_ref) = rest[:N_STREAMS], rest[N_STREAMS:]

    @pl.when(pl.program_id(1) == 0)
    def _():
        x = x_ref[...]
        ms = jnp.mean(x * x, axis=-1, keepdims=True)
        h_ref[...] = (x * lax.rsqrt(ms + NORM_EPS) * g_ref[...]).astype(BF16)

    o_ref[...] = _mm(h_ref[...], jnp.concatenate([r[...] for r in w_refs], axis=0))


def _inproj(x2d, g, w_p, tm=512, tn=2176):
    t, d = x2d.shape
    n = w_p.shape[1]
    piece = d // N_STREAMS
    w_specs = [pl.BlockSpec((piece, tn), functools.partial(lambda q, i, j: (q, j), q)) for q in range(N_STREAMS)]
    return pl.pallas_call(
        _inproj_kernel,
        grid=(t // tm, n // tn),
        in_specs=[pl.BlockSpec((tm, d), lambda i, j: (i, 0)),
                  pl.BlockSpec((1, d), lambda i, j: (0, 0))] + w_specs,
        out_specs=pl.BlockSpec((tm, tn), lambda i, j: (i, j)),
        out_shape=jax.ShapeDtypeStruct((t, n), F32),
        scratch_shapes=[pltpu.VMEM((tm, d), BF16)],
        compiler_params=_params(("parallel", "arbitrary")),
        name="inproj",
    )(x2d, g, *([w_p] * N_STREAMS))


def _softplus(y):
    return jnp.maximum(y, 0.0) + jnp.log(1.0 + jnp.exp(-jnp.abs(y)))


def _rwkv_kernel(tb, r_ref, k_ref, v_ref, wa_ref, gl_ref,
                 mur_ref, muk_ref, muv_ref, muwa_ref, mugl_ref,
                 w0_ref, w2_ref, a0_ref, a2_ref, g2_ref, kk_ref, ka_ref, rk_ref, lg_ref, lb_ref,
                 tri_ref, e_ref, o_ref,
                 s_ref, pr_ref, pk_ref, pv_ref, pwa_ref, pgl_ref,
                 ah_ref, al_ref, bh_ref, bl_ref, kh_ref, kl_ref, rh_ref, rl_ref, vb_ref,
                 cum_ref, y_ref, bonus_ref, g_ref):
    @pl.when(pl.program_id(2) == 0)
    def _():
        s_ref[...] = jnp.zeros_like(s_ref)
        for ref in (pr_ref, pk_ref, pv_ref, pwa_ref, pgl_ref):
            ref[...] = jnp.zeros_like(ref)

    def shift_mix(cur_ref, prev_ref, mu_ref):
        cur = cur_ref[...]
        row = lax.broadcasted_iota(jnp.int32, cur.shape, 0)
        shifted = jnp.where(row == 0, prev_ref[0:1, :], pltpu.roll(cur, 1, axis=0))
        prev_ref[0:1, :] = cur[tb - 1:tb, :]
        return cur + (shifted - cur) * mu_ref[...]

    r = shift_mix(r_ref, pr_ref, mur_ref)
    k = shift_mix(k_ref, pk_ref, muk_ref)
    v = shift_mix(v_ref, pv_ref, muv_ref)
    wa = shift_mix(wa_ref, pwa_ref, muwa_ref)
    gl = shift_mix(gl_ref, pgl_ref, mugl_ref)

    z = w0_ref[...] + _dot3("nn", jnp.tanh(wa[:, :LANE]), w2_ref[...])
    logw = -jnp.exp(-_softplus(-z) - 0.5)
    a = jax.nn.sigmoid(a0_ref[...] + _dot3("nn", wa[:, LANE:], a2_ref[...]))
    g_ref[...] = _bdot("nn", jax.nn.sigmoid(gl), g2_ref[...])

    kk = k * kk_ref[...]
    kk = kk / jnp.maximum(jnp.sqrt(_seg_sum(kk * kk, e_ref)), 1e-12)
    k2 = k * (1.0 + (a - 1.0) * ka_ref[...])
    bonus_ref[...] = _seg_sum(r * k2 * rk_ref[...], e_ref) * v

    lw_hi, lw_lo = _split_bf16(logw)
    cum = _mm(tri_ref[...], lw_hi) + _mm(tri_ref[...], lw_lo)
    inv = jnp.exp(-cum)
    cum_ref[...] = cum
    for val, hi_ref, lo_ref in ((-kk * jnp.exp(cum - logw), ah_ref, al_ref), (kk * a * inv, bh_ref, bl_ref),
                                (k2 * inv, kh_ref, kl_ref), (r * jnp.exp(cum), rh_ref, rl_ref)):
        hi, lo = _split_bf16(val)
        hi_ref[...] = hi
        lo_ref[...] = lo
    vb_ref[...] = v.astype(BF16)

    c2 = 2 * CHUNK
    lane = lax.broadcasted_iota(jnp.int32, (CHUNK, LANE), 1)
    head0 = lane < RWKV_HEAD
    rows = lax.broadcasted_iota(jnp.int32, (c2, c2), 0)
    cols = lax.broadcasted_iota(jnp.int32, (c2, c2), 1)
    strict = rows > cols
    incl = rows >= cols
    eye = (rows == cols).astype(F32)
    zero = jnp.zeros((CHUNK, LANE), BF16)

    def stack(ref, c):
        x = ref[c * CHUNK:(c + 1) * CHUNK, :]
        return jnp.concatenate([jnp.where(head0, x, zero), jnp.where(head0, zero, x)], axis=0)

    chunks = range(tb // CHUNK)
    a_h, b_h, k_h, r_h, v_s = ([stack(ref, c) for c in chunks] for ref in (ah_ref, bh_ref, kh_ref, rh_ref, vb_ref))
    a_l, b_l, k_l, r_l = ([stack(ref, c) for c in chunks] for ref in (al_ref, bl_ref, kl_ref, rl_ref))
    ar_h = [jnp.concatenate([a_h[c], r_h[c]], axis=0) for c in chunks]
    bk_h = [jnp.concatenate([b_h[c], k_h[c]], axis=0) for c in chunks]
    gram = [_bdot("nt", ar_h[c], bk_h[c]) + _bdot("nt", ar_h[c], jnp.concatenate([b_l[c], k_l[c]], axis=0))
            + _bdot("nt", jnp.concatenate([a_l[c], r_l[c]], axis=0), bk_h[c]) for c in chunks]
    l_ba = [jnp.where(strict, gram[c][:c2, :c2], 0.0) for c in chunks]
    l_ka = [jnp.where(strict, gram[c][:c2, c2:], 0.0).astype(BF16) for c in chunks]
    g_b = [jnp.where(incl, gram[c][c2:, :c2], 0.0).astype(BF16) for c in chunks]
    g_k = [jnp.where(incl, gram[c][c2:, c2:], 0.0).astype(BF16) for c in chunks]

    t_inv = [eye + l_ba[c] for c in chunks]
    x = [l_ba[c].astype(BF16) for c in chunks]
    step = 2
    while step < CHUNK:
        x = [_bdot("nn", x[c], x[c]).astype(BF16) for c in chunks]
        t_inv = [t_inv[c] + _bdot("nn", t_inv[c], x[c]) for c in chunks]
        step *= 2

    w2 = [_bdot("nn", l_ka[c], v_s[c]).astype(BF16) for c in chunks]
    p12 = [_bdot("nn", t_inv[c], jnp.concatenate([a_h[c], w2[c]], axis=1)).astype(BF16) for c in chunks]
    qz = [_bdot("nn", g_b[c], p12[c]) for c in chunks]
    q = [(r_h[c].astype(F32) + r_l[c].astype(F32) + qz[c][:, :LANE]).astype(BF16) for c in chunks]
    z_c = [qz[c][:, LANE:] + _bdot("nn", g_k[c], v_s[c]) for c in chunks]
    mn = [_bdot("tn", p12[c], b_h[c]) for c in chunks]
    vk = [_bdot("tn", v_s[c], k_h[c]) for c in chunks]

    s = s_ref[...]
    for c in chunks:
        g_end = jnp.exp(cum_ref[(c + 1) * CHUNK - 1:(c + 1) * CHUNK, :])
        y_st = _bdot("nt", q[c], s) + z_c[c]
        y_ref[c * CHUNK:(c + 1) * CHUNK, :] = y_st[:CHUNK] + y_st[CHUNK:]
        m = (eye + mn[c][:LANE]) * g_end
        n = (mn[c][LANE:] + vk[c]) * g_end
        s = _bdot("nn", s, m) + n
    s_ref[...] = s

    y = y_ref[...]
    inv_n = 1.0 / RWKV_HEAD
    mean = _seg_sum(y, e_ref) * inv_n
    d = y - mean
    var = _seg_sum(d * d, e_ref) * inv_n
    yn = d * lax.rsqrt(var + RWKV_GN_EPS) * lg_ref[...] + lb_ref[...]
    o_ref[...] = ((yn + bonus_ref[...]) * g_ref[...]).astype(o_ref.dtype)


def _rwkv(p, batch, seq, mus, w0, w2p, a0, a2p, g2, k_k, k_a, r_k, lnx_g, lnx_b, tb=512):
    t = p.shape[0]
    w = RWKV_WIDTH
    blocks_per_seq = seq // tb

    def seg(width, base, per_head):
        return pl.BlockSpec((tb, width),
                            lambda b, h, i: (b * blocks_per_seq + i, base // width + (h if per_head else 0)))

    def vec(width, per_head=True):
        return pl.BlockSpec((1, width), lambda b, h, i: (0, h if per_head else 0))

    def mat(rows_):
        return pl.BlockSpec((rows_, LANE), lambda b, h, i: (0, h))

    rows = lax.broadcasted_iota(jnp.int32, (tb, tb), 0)
    cols = lax.broadcasted_iota(jnp.int32, (tb, tb), 1)
    tri = ((rows // CHUNK == cols // CHUNK) & (rows >= cols)).astype(BF16)
    rows = lax.broadcasted_iota(jnp.int32, (LANE, LANE), 0)
    cols = lax.broadcasted_iota(jnp.int32, (LANE, LANE), 1)
    e_seg = (rows // RWKV_HEAD == cols // RWKV_HEAD).astype(BF16)

    in_specs = [seg(LANE, COL_R, True), seg(LANE, COL_K, True), seg(LANE, COL_V, True),
                seg(2 * LANE, COL_WA, False), seg(GATE_LORA, COL_GL, False),
                vec(LANE), vec(LANE), vec(LANE), vec(2 * LANE, False), vec(GATE_LORA, False),
                vec(LANE), mat(LANE), vec(LANE), mat(LANE), mat(GATE_LORA), vec(LANE), vec(LANE), vec(LANE),
                vec(LANE), vec(LANE),
                pl.BlockSpec((tb, tb), lambda b, h, i: (0, 0)),
                pl.BlockSpec((LANE, LANE), lambda b, h, i: (0, 0))]
    half = pltpu.VMEM((tb, LANE), BF16)
    full = pltpu.VMEM((tb, LANE), F32)
    return pl.pallas_call(
        functools.partial(_rwkv_kernel, tb),
        grid=(batch, w // LANE, blocks_per_seq),
        in_specs=in_specs,
        out_specs=pl.BlockSpec((tb, LANE), lambda b, h, i: (b * blocks_per_seq + i, h)),
        out_shape=jax.ShapeDtypeStruct((t, w), BF16),
        scratch_shapes=[pltpu.VMEM((LANE, LANE), F32),
                        pltpu.VMEM((8, LANE), F32), pltpu.VMEM((8, LANE), F32), pltpu.VMEM((8, LANE), F32),
                        pltpu.VMEM((8, 2 * LANE), F32), pltpu.VMEM((8, GATE_LORA), F32)]
                       + [half] * 9 + [full] * 4,
        compiler_params=_params(("parallel", "parallel", "arbitrary")),
        name="rwkv",
    )(p, p, p, p, p, *mus, w0, w2p, a0, a2p, g2, k_k, k_a, r_k, lnx_g, lnx_b, tri, e_seg)


def _gla_kernel(n_chunks, q_ref, k_ref, v_ref, og_ref, al_ref, wal_ref, bal_ref, ng_ref, o_ref, s_ref):
    @pl.when(pl.program_id(2) == 0)
    def _():
        s_ref[...] = jnp.zeros_like(s_ref)

    n_sub = CHUNK // SUB
    scale = GLA_DK ** -0.5
    rows = lax.broadcasted_iota(jnp.int32, (CHUNK, CHUNK), 0)
    cols = lax.broadcasted_iota(jnp.int32, (CHUNK, CHUNK), 1)
    tri = (rows >= cols).astype(BF16)
    srow = lax.broadcasted_iota(jnp.int32, (CHUNK, LANE), 0)
    slane = lax.broadcasted_iota(jnp.int32, (CHUNK, LANE), 1)

    def body(c, carry):
        sl = pl.ds(pl.multiple_of(c * CHUNK, CHUNK), CHUNK)
        q = q_ref[sl, :] * scale
        k = k_ref[sl, :]
        v = v_ref[sl, :]
        pre = _dot3("nn", al_ref[sl, :], wal_ref[...]) + bal_ref[...]
        g = jax.nn.log_sigmoid(pre) / GLA_TAU
        g_hi, g_lo = _split_bf16(g)
        b = _mm(tri, g_hi) + _mm(tri, g_lo)

        pieces = []
        for i in range(n_sub):
            lo, hi = i * SUB, (i + 1) * SUB
            base = b[lo - 1:lo, :] if i > 0 else jnp.zeros((1, LANE), F32)
            q_i = q[lo:hi, :] * jnp.exp(b[lo:hi, :] - base)
            if i > 0:
                k_hat = k[:lo, :] * jnp.exp(base - b[:lo, :])
                top = _bdot("nt", k_hat, q_i)
                pieces.append(jnp.concatenate([top, jnp.zeros((CHUNK - lo, SUB), F32)], axis=0))
            else:
                pieces.append(jnp.zeros((CHUNK, SUB), F32))
        pieces.append(jnp.zeros((CHUNK, LANE - CHUNK), F32))
        a_t = jnp.concatenate(pieces, axis=1)

        for tau in range(SUB):
            q_b = jnp.concatenate([jnp.broadcast_to(q[i * SUB + tau:i * SUB + tau + 1, :], (SUB, LANE))
                                   for i in range(n_sub)], axis=0)
            b_b = jnp.concatenate([jnp.broadcast_to(b[i * SUB + tau:i * SUB + tau + 1, :], (SUB, LANE))
                                   for i in range(n_sub)], axis=0)
            ok = (srow % SUB) <= tau
            dec = jnp.where(ok, jnp.exp(jnp.where(ok, b_b - b, 0.0)), 0.0)
            col = jnp.sum(k * q_b * dec, axis=-1, keepdims=True)
            hit = (slane % SUB == tau) & (slane // SUB == srow // SUB)
            a_t = a_t + jnp.where(hit, col, 0.0)

        s = s_ref[...]
        o = _bdot("tn", a_t[:, :CHUNK], v) + _bdot("nt", q * jnp.exp(b), s)
        b_last = b[CHUNK - 1:CHUNK, :]
        s_ref[...] = s * jnp.exp(b_last) + _bdot("tn", v, k * jnp.exp(b_last - b))

        o = o * lax.rsqrt(jnp.mean(o * o, axis=-1, keepdims=True) + NORM_EPS) * ng_ref[...]
        og = og_ref[sl, :]
        o_ref[sl, :] = (o * (og * jax.nn.sigmoid(og))).astype(o_ref.dtype)
        return carry

    lax.fori_loop(0, n_chunks, body, 0)


def _gla(p, w_alpha2p, b_alpha, norm_g, batch, seq, tb=256):
    t = p.shape[0]
    blocks_per_seq = seq // tb

    def col(width, base):
        return pl.BlockSpec((tb, width), lambda b, h, i: (b * blocks_per_seq + i, base // width + h))

    return pl.pallas_call(
        functools.partial(_gla_kernel, tb // CHUNK),
        grid=(batch, GLA_HEADS, blocks_per_seq),
        in_specs=[col(GLA_DK, COL_GQ), col(GLA_DK, COL_GK), col(GLA_DV, COL_GV), col(GLA_DV, COL_GOG),
                  pl.BlockSpec((tb, LANE), lambda b, h, i: (b * blocks_per_seq + i, COL_GAL // LANE)),
                  pl.BlockSpec((LANE, GLA_DK), lambda b, h, i: (0, h)),
                  pl.BlockSpec((1, GLA_DK), lambda b, h, i: (0, h)),
                  pl.BlockSpec((1, GLA_DV), lambda b, h, i: (0, 0))],
        out_specs=pl.BlockSpec((tb, GLA_DV), lambda b, h, i: (b * blocks_per_seq + i, h)),
        out_shape=jax.ShapeDtypeStruct((t, GLA_VAL_WIDTH), BF16),
        scratch_shapes=[pltpu.VMEM((GLA_DV, GLA_DK), F32)],
        compiler_params=_params(("parallel", "parallel", "arbitrary")),
        name="gla",
    )(p, p, p, p, p, w_alpha2p, b_alpha, norm_g)


def _merge_kernel(ya_ref, yb_ref, pg_ref, bm_ref, wa_ref, wb_ref, o_ref):
    d = wa_ref.shape[1]
    gates = jax.nn.sigmoid(pg_ref[...] + bm_ref[...])
    ma = _mm(ya_ref[...], wa_ref[...])
    mb = _mm(yb_ref[...], wb_ref[...])
    o_ref[...] = (gates[:, :d] * ma + gates[:, d:] * mb).astype(o_ref.dtype)


def _merge(ya, yb, p, b_merge, wa, wb, tm=256):
    t = ya.shape[0]
    d = wa.shape[1]
    return pl.pallas_call(
        _merge_kernel,
        grid=(t // tm,),
        in_specs=[pl.BlockSpec((tm, ya.shape[1]), lambda i: (i, 0)),
                  pl.BlockSpec((tm, yb.shape[1]), lambda i: (i, 0)),
                  pl.BlockSpec((tm, 2 * d), lambda i: (i, COL_GATE // (2 * d))),
                  pl.BlockSpec((1, 2 * d), lambda i: (0, 0)),
                  pl.BlockSpec(wa.shape, lambda i: (0, 0)),
                  pl.BlockSpec(wb.shape, lambda i: (0, 0))],
        out_specs=pl.BlockSpec((tm, d), lambda i: (i, 0)),
        out_shape=jax.ShapeDtypeStruct((t, d), BF16),
        compiler_params=_params(("parallel",)),
        name="merge",
    )(ya, yb, p, b_merge, wa, wb)


def _outproj_kernel(m_ref, x_ref, wo_ref, g_ref, wr_ref, br_ref, x1_ref, h_ref, lg_ref):
    x1 = x_ref[...] + _mm(m_ref[...], wo_ref[...])
    x1_ref[...] = x1
    ms = jnp.mean(x1 * x1, axis=-1, keepdims=True)
    h = x1 * lax.rsqrt(ms + NORM_EPS) * g_ref[...]
    h_ref[...] = h
    lg_ref[...] = _mm(h, wr_ref[...], HI) + br_ref[...]


def _outproj(merged, x2d, w_out, g_ffn, w_router_p, b_router_p, tm=256):
    t, d = x2d.shape
    row = pl.BlockSpec((tm, d), lambda i: (i, 0))
    return pl.pallas_call(
        _outproj_kernel,
        grid=(t // tm,),
        in_specs=[row, row,
                  pl.BlockSpec((d, d), lambda i: (0, 0)),
                  pl.BlockSpec((1, d), lambda i: (0, 0)),
                  pl.BlockSpec((d, LANE), lambda i: (0, 0)),
                  pl.BlockSpec((1, LANE), lambda i: (0, 0))],
        out_specs=[row, row, pl.BlockSpec((tm, LANE), lambda i: (i, 0))],
        out_shape=[jax.ShapeDtypeStruct((t, d), F32), jax.ShapeDtypeStruct((t, d), F32),
                   jax.ShapeDtypeStruct((t, LANE), F32)],
        compiler_params=_params(("parallel",)),
        name="outproj",
    )(merged, x2d, w_out, g_ffn, w_router_p, b_router_p)


def _row_copy(src_ref, src_row, dst_ref, dst_row, sem):
    return pltpu.make_async_copy(src_ref.at[pl.ds(src_row, 1)], dst_ref.at[pl.ds(dst_row, 1)], sem)


def _dispatch_kernel(tm, dest_ref, h_ref, xs_in_ref, xs_ref, sem):
    del xs_in_ref

    def issue(r, carry):
        for j in range(TOP_K):
            _row_copy(h_ref, r, xs_ref, dest_ref[0, 0, r * TOP_K + j], sem).start()
        return carry

    lax.fori_loop(0, tm, issue, 0)

    def drain(r, carry):
        for j in range(TOP_K):
            _row_copy(h_ref, r, xs_ref, dest_ref[0, 0, r * TOP_K + j], sem).wait()
        return carry

    lax.fori_loop(0, tm, drain, 0)


def _dispatch(h, dest, n_rows, tm=256):
    t, d = h.shape
    xs0 = jnp.zeros((n_rows, d), h.dtype)
    dest3 = dest.reshape(t // tm, 1, tm * TOP_K)
    return pl.pallas_call(
        functools.partial(_dispatch_kernel, tm),
        grid=(t // tm,),
        in_specs=[pl.BlockSpec((1, 1, tm * TOP_K), lambda i: (i, 0, 0), memory_space=pltpu.SMEM),
                  pl.BlockSpec((tm, d), lambda i: (i, 0)),
                  pl.BlockSpec(memory_space=pl.ANY)],
        out_specs=pl.BlockSpec(memory_space=pl.ANY),
        out_shape=jax.ShapeDtypeStruct((n_rows, d), h.dtype),
        scratch_shapes=[pltpu.SemaphoreType.DMA(())],
        input_output_aliases={2: 0},
        compiler_params=_params(("arbitrary",)),
        name="dispatch",
    )(dest3, h, xs0)


MOE_TILE = 512
MOE_SUB = 256
MOE_SLAB = 512


def _expert_kernel(n_slab, te_ref, ns_ref, nv_ref, x_ref, bg_ref, bu_ref, bd_ref, *rest):
    wg_refs, wu_refs, wd_refs = (rest[q * N_STREAMS:(q + 1) * N_STREAMS] for q in range(3))
    o_ref, xb_ref, gate_ref, up_ref = rest[3 * N_STREAMS:]

    def slab(refs):
        return jnp.concatenate([r[0].astype(BF16) for r in refs], axis=0)

    w = pl.program_id(0)
    s = pl.program_id(1)
    n_sub = ns_ref[w]
    kt = MOE_SLAB

    @pl.when((s == 0) & (n_sub > 0))
    def _():
        for j in range(n_slab):
            xb_ref[j] = x_ref[:, j * kt:(j + 1) * kt].astype(BF16)
            gate_ref[j] = jnp.broadcast_to(bg_ref[0, :, j * kt:(j + 1) * kt], gate_ref.shape[1:])
            up_ref[j] = jnp.broadcast_to(bu_ref[0, :, j * kt:(j + 1) * kt], up_ref.shape[1:])

    @pl.when(s == n_slab)
    def _():
        o_ref[...] = jnp.broadcast_to(bd_ref[0], o_ref.shape)

    @pl.when(s < n_slab)
    def _():
        def sub(i, carry):
            rows = pl.ds(pl.multiple_of(i * MOE_SUB, MOE_SUB), MOE_SUB)
            xb = xb_ref[s, rows, :]
            gate = _mm(xb, slab(wg_refs))
            up = _mm(xb, slab(wu_refs))
            for j in range(n_slab):
                gate_ref[j, rows, :] += gate[:, j * kt:(j + 1) * kt]
                up_ref[j, rows, :] += up[:, j * kt:(j + 1) * kt]
            return carry

        lax.fori_loop(0, n_sub, sub, 0)

    @pl.when(s >= n_slab)
    def _():
        j = s - n_slab

        def sub(i, carry):
            rows = pl.ds(pl.multiple_of(i * MOE_SUB, MOE_SUB), MOE_SUB)
            gate = jnp.minimum(gate_ref[j, rows, :], SWIGLU_LIMIT)
            up = jnp.clip(up_ref[j, rows, :], -SWIGLU_LIMIT, SWIGLU_LIMIT)
            glu = gate * jax.nn.sigmoid(gate * SWIGLU_ALPHA)
            act = ((up + 1.0) * glu).astype(BF16)
            o_ref[rows, :] += _mm(act, slab(wd_refs))
            return carry

        lax.fori_loop(0, n_sub, sub, 0)


def _experts(xs, tile_expert, tile_nsub, n_valid, w_gate, b_gate, w_up, b_up, w_down, b_down):
    n_rows, d = xs.shape
    n_e, _, d_ff = w_gate.shape
    tm, kt = MOE_TILE, MOE_SLAB
    n_tiles = n_rows // tm
    n_slab = d // kt
    assert d == d_ff and d % kt == 0

    def tile(w, nv):
        return jnp.minimum(w, nv[0] - 1)

    def up_slab(w, s, ns):
        return jnp.where(ns[w] > 0, jnp.minimum(s, n_slab - 1), n_slab - 1)

    def down_slab(w, s, ns):
        return jnp.where(ns[w] > 0, jnp.maximum(s - n_slab, 0), n_slab - 1)

    def piece_spec(slab_of, q):
        return pl.BlockSpec((1, kt // N_STREAMS, d),
                            lambda w, s, te, ns, nv: (te[w], slab_of(w, s, ns) * N_STREAMS + q, 0))

    grid_spec = pltpu.PrefetchScalarGridSpec(
        num_scalar_prefetch=3,
        grid=(n_tiles, 2 * n_slab),
        in_specs=[pl.BlockSpec((tm, d), lambda w, s, te, ns, nv: (tile(w, nv), 0)),
                  pl.BlockSpec((1, 1, d_ff), lambda w, s, te, ns, nv: (te[w], 0, 0)),
                  pl.BlockSpec((1, 1, d_ff), lambda w, s, te, ns, nv: (te[w], 0, 0)),
                  pl.BlockSpec((1, 1, d), lambda w, s, te, ns, nv: (te[w], 0, 0))]
                 + [piece_spec(up_slab, q) for q in range(N_STREAMS)] * 2
                 + [piece_spec(down_slab, q) for q in range(N_STREAMS)],
        out_specs=pl.BlockSpec((tm, d), lambda w, s, te, ns, nv: (w, 0)),
        scratch_shapes=[pltpu.VMEM((n_slab, tm, kt), BF16), pltpu.VMEM((n_slab, tm, kt), F32),
                        pltpu.VMEM((n_slab, tm, kt), F32)],
    )
    return pl.pallas_call(
        functools.partial(_expert_kernel, n_slab),
        grid_spec=grid_spec,
        out_shape=jax.ShapeDtypeStruct((n_rows, d), F32),
        compiler_params=_params(("arbitrary", "arbitrary")),
        name="experts",
    )(tile_expert, tile_nsub, n_valid, xs, b_gate.reshape(n_e, 1, d_ff), b_up.reshape(n_e, 1, d_ff),
      b_down.reshape(n_e, 1, d), *([w_gate] * N_STREAMS), *([w_up] * N_STREAMS), *([w_down] * N_STREAMS))


def _expert_cols_kernel(te_ref, ns_ref, nv_ref, x_ref, wg_ref, bg_ref, wu_ref, bu_ref, wd_ref, bd_ref, o_ref, xb_ref):
    w = pl.program_id(0)

    @pl.when(pl.program_id(1) == 0)
    def _():
        xb_ref[...] = x_ref[...].astype(BF16)
        o_ref[...] = jnp.broadcast_to(bd_ref[0], o_ref.shape)

    @pl.when(ns_ref[w] > 0)
    def _():
        xb = xb_ref[...]
        gate = _mm(xb, wg_ref[0].astype(BF16)) + bg_ref[0]
        up = _mm(xb, wu_ref[0].astype(BF16)) + bu_ref[0]
        gate = jnp.minimum(gate, SWIGLU_LIMIT)
        up = jnp.clip(up, -SWIGLU_LIMIT, SWIGLU_LIMIT)
        glu = gate * jax.nn.sigmoid(gate * SWIGLU_ALPHA)
        act = ((up + 1.0) * glu).astype(BF16)
        o_ref[...] += _mm(act, wd_ref[0].astype(BF16))


def _experts_cols(xs, tile_expert, tile_nsub, n_valid, w_gate, b_gate, w_up, b_up, w_down, b_down, tf=512):
    n_rows, d = xs.shape
    n_e, _, d_ff = w_gate.shape
    tm = MOE_TILE
    n_tiles = n_rows // tm
    n_f = d_ff // tf

    def tile(w, nv):
        return jnp.minimum(w, nv[0] - 1)

    def fidx(w, f, ns):
        return jnp.where(ns[w] > 0, f, n_f - 1)

    grid_spec = pltpu.PrefetchScalarGridSpec(
        num_scalar_prefetch=3,
        grid=(n_tiles, n_f),
        in_specs=[pl.BlockSpec((tm, d), lambda w, f, te, ns, nv: (tile(w, nv), 0)),
                  pl.BlockSpec((1, d, tf), lambda w, f, te, ns, nv: (te[w], 0, fidx(w, f, ns))),
                  pl.BlockSpec((1, 1, tf), lambda w, f, te, ns, nv: (te[w], 0, fidx(w, f, ns))),
                  pl.BlockSpec((1, d, tf), lambda w, f, te, ns, nv: (te[w], 0, fidx(w, f, ns))),
                  pl.BlockSpec((1, 1, tf), lambda w, f, te, ns, nv: (te[w], 0, fidx(w, f, ns))),
                  pl.BlockSpec((1, tf, d), lambda w, f, te, ns, nv: (te[w], fidx(w, f, ns), 0)),
                  pl.BlockSpec((1, 1, d), lambda w, f, te, ns, nv: (te[w], 0, 0))],
        out_specs=pl.BlockSpec((tm, d), lambda w, f, te, ns, nv: (w, 0)),
        scratch_shapes=[pltpu.VMEM((tm, d), BF16)],
    )
    return pl.pallas_call(
        _expert_cols_kernel,
        grid_spec=grid_spec,
        out_shape=jax.ShapeDtypeStruct((n_rows, d), F32),
        compiler_params=_params(("arbitrary", "arbitrary")),
        name="experts",
    )(tile_expert, tile_nsub, n_valid, xs, w_gate, b_gate.reshape(n_e, 1, d_ff), w_up,
      b_up.reshape(n_e, 1, d_ff), w_down, b_down.reshape(n_e, 1, d))


def _combine_kernel(tm, final, pos_ref, eo_ref, x1_ref, wt_ref, g_ref, o_ref, buf_ref, sem):
    def issue(r, carry):
        for j in range(TOP_K):
            _row_copy(eo_ref, pos_ref[0, 0, r * TOP_K + j], buf_ref.at[j], r, sem).start()
        return carry

    lax.fori_loop(0, tm, issue, 0)

    def drain(r, carry):
        for j in range(TOP_K):
            _row_copy(eo_ref, pos_ref[0, 0, r * TOP_K + j], buf_ref.at[j], r, sem).wait()
        return carry

    lax.fori_loop(0, tm, drain, 0)

    wt = wt_ref[...]
    y = x1_ref[...]
    for j in range(TOP_K):
        y = y + buf_ref[j] * wt[:, j:j + 1]
    if final:
        y = y * lax.rsqrt(jnp.mean(y * y, axis=-1, keepdims=True) + NORM_EPS) * g_ref[...]
    o_ref[...] = y


def _combine(eo, pos, x1, top_w, g_final, final, tm=128):
    t, d = x1.shape
    pos3 = pos.reshape(t // tm, 1, tm * TOP_K)
    return pl.pallas_call(
        functools.partial(_combine_kernel, tm, final),
        grid=(t // tm,),
        in_specs=[pl.BlockSpec((1, 1, tm * TOP_K), lambda i: (i, 0, 0), memory_space=pltpu.SMEM),
                  pl.BlockSpec(memory_space=pl.ANY),
                  pl.BlockSpec((tm, d), lambda i: (i, 0)),
                  pl.BlockSpec((tm, TOP_K), lambda i: (i, 0)),
                  pl.BlockSpec((1, d), lambda i: (0, 0))],
        out_specs=pl.BlockSpec((tm, d), lambda i: (i, 0)),
        out_shape=jax.ShapeDtypeStruct((t, d), F32),
        scratch_shapes=[pltpu.VMEM((TOP_K, tm, d), F32), pltpu.SemaphoreType.DMA(())],
        compiler_params=_params(("arbitrary",)),
        name="combine",
    )(pos3, eo, x1, top_w, g_final)


def _routing(logits):
    t = logits.shape[0]
    tm = MOE_TILE
    top_vals, top_idx = lax.top_k(logits[:, :N_EXPERTS], TOP_K)
    top_w = jax.nn.softmax(top_vals, axis=-1)
    e_flat = top_idx.reshape(-1).astype(jnp.int32)
    onehot = (e_flat[:, None] == jnp.arange(N_EXPERTS, dtype=jnp.int32)[None, :]).astype(jnp.int32)
    incl = jnp.cumsum(onehot, axis=0)
    counts = incl[-1]
    rank = jnp.sum((incl - onehot) * onehot, axis=1)
    tiles = (counts + tm - 1) // tm
    tile_end = jnp.cumsum(tiles)
    tile_start = tile_end - tiles
    dest = jnp.sum(onehot * (tile_start * tm)[None, :], axis=1) + rank
    n_tiles = (t * TOP_K) // tm + N_EXPERTS
    wid = jnp.arange(n_tiles, dtype=jnp.int32)
    n_valid = tile_end[-1:].astype(jnp.int32)
    tile_expert = jnp.minimum(jnp.searchsorted(tile_end, jnp.minimum(wid, n_valid[0] - 1), side="right"),
                              N_EXPERTS - 1).astype(jnp.int32)
    tile_rows = jnp.clip(counts[tile_expert] - (wid - tile_start[tile_expert]) * tm, 0, tm)
    tile_nsub = jnp.where(wid < n_valid[0], (tile_rows + MOE_SUB - 1) // MOE_SUB, 0).astype(jnp.int32)
    return top_w, dest.astype(jnp.int32), tile_expert, tile_nsub, n_valid, n_tiles * tm


def _pad_cols(a, width):
    return jnp.pad(a, ((0, 0), (0, width - a.shape[1])))


def _pad_rows(a, height):
    return jnp.pad(a, ((0, height - a.shape[0]), (0, 0)))


def _split_cols(a, sizes):
    out, acc = [], 0
    for s in sizes:
        out.append(a[..., acc:acc + s])
        acc += s
    return out


def kernel(x, norm_mix_g, w_in, token_shift_mu, rwkv_w0, rwkv_w2, rwkv_a0, rwkv_a2, rwkv_g2, rwkv_k_k, rwkv_k_a, rwkv_r_k, rwkv_lnx_g, rwkv_lnx_b, gla_w_alpha2, gla_b_alpha, gla_norm_g, b_merge, w_branch_rwkv, w_branch_gla, w_out, norm_ffn_g, w_router, b_router, w_gate, b_gate, w_up, b_up, w_down, b_down, norm_final_g):
    batch, seq, d = x.shape
    t = batch * seq
    depth = w_in.shape[0]
    w = RWKV_WIDTH
    sizes = (w, w, w, DECAY_LORA, ICL_LORA, GATE_LORA, GLA_KEY_WIDTH, GLA_KEY_WIDTH, GLA_VAL_WIDTH,
             GLA_VAL_WIDTH, GLA_GATE_LORA, 2 * d)

    x2d = x.reshape(t, d)
    for l in range(depth):
        s_r, s_k, s_v, s_wl, s_al, s_gl, s_gq, s_gk, s_gv, s_gog, s_gal, s_gate = _split_cols(w_in[l], sizes)
        w_p = jnp.concatenate(
            [s_gate, s_r, s_k, s_v, _pad_cols(s_wl, LANE), _pad_cols(s_al, LANE), s_gl, s_gq, s_gk, s_gv,
             s_gog, _pad_cols(s_gal, LANE)], axis=1).astype(BF16)
        mu = token_shift_mu[l][None, :]
        m_r, m_k, m_v, m_wl, m_al, m_gl = _split_cols(mu, sizes[:6])
        mus = (m_r, m_k, m_v, jnp.concatenate([_pad_cols(m_wl, LANE), _pad_cols(m_al, LANE)], axis=1), m_gl)

        p = _inproj(x2d, norm_mix_g[l][None, :], w_p)

        ya = _rwkv(p, batch, seq, mus, rwkv_w0[l][None, :], _pad_rows(rwkv_w2[l], LANE), rwkv_a0[l][None, :],
                   _pad_rows(rwkv_a2[l], LANE), rwkv_g2[l], rwkv_k_k[l][None, :], rwkv_k_a[l][None, :],
                   rwkv_r_k[l].reshape(1, w), rwkv_lnx_g[l][None, :], rwkv_lnx_b[l][None, :])

        yb = _gla(p, _pad_rows(gla_w_alpha2[l], LANE), gla_b_alpha[l][None, :], gla_norm_g[l][None, :],
                  batch, seq)

        merged = _merge(ya, yb, p, b_merge[l][None, :], w_branch_rwkv[l].astype(BF16),
                        w_branch_gla[l].astype(BF16))
        b_router_p = jnp.concatenate([b_router[l], jnp.full((LANE - N_EXPERTS,), -1e30, F32)])[None, :]
        x1, h2, logits = _outproj(merged, x2d, w_out[l].astype(BF16), norm_ffn_g[l][None, :],
                                  _pad_cols(w_router[l], LANE), b_router_p)

        top_w, dest, tile_expert, tile_nsub, n_valid, n_rows = _routing(logits)
        xs = _dispatch(h2, dest, n_rows)
        eo = _experts_cols(xs, tile_expert, tile_nsub, n_valid, w_gate[l], b_gate[l], w_up[l], b_up[l],
                           w_down[l], b_down[l])
        final = l == depth - 1
        x2d = _combine(eo, dest, x1, top_w, norm_final_g[None, :], final)
    return x2d.reshape(batch, seq, d)
```

```python
import functools

import jax
import jax.numpy as jnp
from jax import lax
from jax.experimental import pallas as pl
from jax.experimental.pallas import tpu as pltpu

F32 = jnp.float32
BF16 = jnp.bfloat16
HI = lax.Precision.HIGHEST

NORM_EPS = 1e-5
RWKV_GN_EPS = 64e-5
RWKV_WIDTH = 1024
RWKV_HEAD = 64
DECAY_LORA = 96
ICL_LORA = 96
GATE_LORA = 256
GLA_HEADS = 4
GLA_DK = 128
GLA_DV = 256
GLA_KEY_WIDTH = GLA_HEADS * GLA_DK
GLA_VAL_WIDTH = GLA_HEADS * GLA_DV
GLA_GATE_LORA = 16
GLA_TAU = 16.0
N_EXPERTS = 32
TOP_K = 4
SWIGLU_LIMIT = 7.0
SWIGLU_ALPHA = 1.702

CHUNK = 64
SUB = 16
LANE = 128

COL_GATE = 0
COL_R = 4096
COL_K = 5120
COL_V = 6144
COL_WA = 7168
COL_GL = 7424
COL_GQ = 7680
COL_GK = 8192
COL_GV = 8704
COL_GOG = 9728
COL_GAL = 10752
N_PROJ = 10880

VMEM_LIMIT = 56 * 1024 * 1024

_DIMS = {"nn": (((1,), (0,)), ((), ())), "nt": (((1,), (1,)), ((), ())), "tn": (((0,), (0,)), ((), ()))}


def _mm(a, b, prec=None):
    return jnp.dot(a, b, preferred_element_type=F32, precision=prec)


def _bdot(kind, a, b):
    return lax.dot_general(a.astype(BF16), b.astype(BF16), _DIMS[kind], preferred_element_type=F32)


def _split_bf16(a):
    hi = a.astype(BF16)
    return hi, (a - hi.astype(F32)).astype(BF16)


def _dot3(kind, a, b):
    ah, al = _split_bf16(a)
    bh, bl = _split_bf16(b)
    return _bdot(kind, ah, bh) + _bdot(kind, ah, bl) + _bdot(kind, al, bh)


def _seg_sum(x, e_ref):
    hi, lo = _split_bf16(x)
    e = e_ref[...]
    return _mm(hi, e) + _mm(lo, e)


def _params(sem, vmem=VMEM_LIMIT):
    return pltpu.CompilerParams(dimension_semantics=sem, vmem_limit_bytes=vmem)


N_STREAMS = 4


def _inproj_kernel(x_ref, g_ref, *rest):
    w_refs, (o_ref, h_ref) = rest[:N_STREAMS], rest[N_STREAMS:]

    @pl.when(pl.program_id(1) == 0)
    def _():
        x = x_ref[...]
        ms = jnp.mean(x * x, axis=-1, keepdims=True)
        h_ref[...] = (x * lax.rsqrt(ms + NORM_EPS) * g_ref[...]).astype(BF16)

    o_ref[...] = _mm(h_ref[...], jnp.concatenate([r[...] for r in w_refs], axis=0))


def _inproj(x2d, g, w_p, tm=512, tn=2176):
    t, d = x2d.shape
    n = w_p.shape[1]
    piece = d // N_STREAMS
    w_specs = [pl.BlockSpec((piece, tn), functools.partial(lambda q, i, j: (q, j), q)) for q in range(N_STREAMS)]
    return pl.pallas_call(
        _inproj_kernel,
        grid=(t // tm, n // tn),
        in_specs=[pl.BlockSpec((tm, d), lambda i, j: (i, 0)),
                  pl.BlockSpec((1, d), lambda i, j: (0, 0))] + w_specs,
        out_specs=pl.BlockSpec((tm, tn), lambda i, j: (i, j)),
        out_shape=jax.ShapeDtypeStruct((t, n), F32),
        scratch_shapes=[pltpu.VMEM((tm, d), BF16)],
        compiler_params=_params(("parallel", "arbitrary")),
        name="inproj",
    )(x2d, g, *([w_p] * N_STREAMS))


def _softplus(y):
    return jnp.maximum(y, 0.0) + jnp.log(1.0 + jnp.exp(-jnp.abs(y)))


def _rwkv_kernel(tb, r_ref, k_ref, v_ref, wa_ref, gl_ref,
                 mur_ref, muk_ref, muv_ref, muwa_ref, mugl_ref,
                 w0_ref, w2_ref, a0_ref, a2_ref, g2_ref, kk_ref, ka_ref, rk_ref, lg_ref, lb_ref,
                 tri_ref, e_ref, o_ref,
                 s_ref, pr_ref, pk_ref, pv_ref, pwa_ref, pgl_ref,
                 ah_ref, al_ref, bh_ref, bl_ref, kh_ref, kl_ref, rh_ref, rl_ref, vb_ref,
                 cum_ref, y_ref, bonus_ref, g_ref):
    @pl.when(pl.program_id(2) == 0)
    def _():
        s_ref[...] = jnp.zeros_like(s_ref)
        for ref in (pr_ref, pk_ref, pv_ref, pwa_ref, pgl_ref):
            ref[...] = jnp.zeros_like(ref)

    def shift_mix(cur_ref, prev_ref, mu_ref):
        cur = cur_ref[...]
        row = lax.broadcasted_iota(jnp.int32, cur.shape, 0)
        shifted = jnp.where(row == 0, prev_ref[0:1, :], pltpu.roll(cur, 1, axis=0))
        prev_ref[0:1, :] = cur[tb - 1:tb, :]
        return cur + (shifted - cur) * mu_ref[...]

    r = shift_mix(r_ref, pr_ref, mur_ref)
    k = shift_mix(k_ref, pk_ref, muk_ref)
    v = shift_mix(v_ref, pv_ref, muv_ref)
    wa = shift_mix(wa_ref, pwa_ref, muwa_ref)
    gl = shift_mix(gl_ref, pgl_ref, mugl_ref)

    z = w0_ref[...] + _dot3("nn", jnp.tanh(wa[:, :LANE]), w2_ref[...])
    logw = -jnp.exp(-_softplus(-z) - 0.5)
    a = jax.nn.sigmoid(a0_ref[...] + _dot3("nn", wa[:, LANE:], a2_ref[...]))
    g_ref[...] = _bdot("nn", jax.nn.sigmoid(gl), g2_ref[...])

    kk = k * kk_ref[...]
    kk = kk / jnp.maximum(jnp.sqrt(_seg_sum(kk * kk, e_ref)), 1e-12)
    k2 = k * (1.0 + (a - 1.0) * ka_ref[...])
    bonus_ref[...] = _seg_sum(r * k2 * rk_ref[...], e_ref) * v

    lw_hi, lw_lo = _split_bf16(logw)
    cum = _mm(tri_ref[...], lw_hi) + _mm(tri_ref[...], lw_lo)
    inv = jnp.exp(-cum)
    cum_ref[...] = cum
    for val, hi_ref, lo_ref in ((-kk * jnp.exp(cum - logw), ah_ref, al_ref), (kk * a * inv, bh_ref, bl_ref),
                                (k2 * inv, kh_ref, kl_ref), (r * jnp.exp(cum), rh_ref, rl_ref)):
        hi, lo = _split_bf16(val)
        hi_ref[...] = hi
        lo_ref[...] = lo
    vb_ref[...] = v.astype(BF16)

    c2 = 2 * CHUNK
    lane = lax.broadcasted_iota(jnp.int32, (CHUNK, LANE), 1)
    head0 = lane < RWKV_HEAD
    rows = lax.broadcasted_iota(jnp.int32, (c2, c2), 0)
    cols = lax.broadcasted_iota(jnp.int32, (c2, c2), 1)
    strict = rows > cols
    incl = rows >= cols
    eye = (rows == cols).astype(F32)
    zero = jnp.zeros((CHUNK, LANE), BF16)

    def stack(ref, c):
        x = ref[c * CHUNK:(c + 1) * CHUNK, :]
        return jnp.concatenate([jnp.where(head0, x, zero), jnp.where(head0, zero, x)], axis=0)

    chunks = range(tb // CHUNK)
    a_h, b_h, k_h, r_h, v_s = ([stack(ref, c) for c in chunks] for ref in (ah_ref, bh_ref, kh_ref, rh_ref, vb_ref))
    a_l, b_l, k_l, r_l = ([stack(ref, c) for c in chunks] for ref in (al_ref, bl_ref, kl_ref, rl_ref))
    ar_h = [jnp.concatenate([a_h[c], r_h[c]], axis=0) for c in chunks]
    bk_h = [jnp.concatenate([b_h[c], k_h[c]], axis=0) for c in chunks]
    gram = [_bdot("nt", ar_h[c], bk_h[c]) + _bdot("nt", ar_h[c], jnp.concatenate([b_l[c], k_l[c]], axis=0))
            + _bdot("nt", jnp.concatenate([a_l[c], r_l[c]], axis=0), bk_h[c]) for c in chunks]
    l_ba = [jnp.where(strict, gram[c][:c2, :c2], 0.0) for c in chunks]
    l_ka = [jnp.where(strict, gram[c][:c2, c2:], 0.0).astype(BF16) for c in chunks]
    g_b = [jnp.where(incl, gram[c][c2:, :c2], 0.0).astype(BF16) for c in chunks]
    g_k = [jnp.where(incl, gram[c][c2:, c2:], 0.0).astype(BF16) for c in chunks]

    t_inv = [eye + l_ba[c] for c in chunks]
    x = [l_ba[c].astype(BF16) for c in chunks]
    step = 2
    while step < CHUNK:
        x = [_bdot("nn", x[c], x[c]).astype(BF16) for c in chunks]
        t_inv = [t_inv[c] + _bdot("nn", t_inv[c], x[c]) for c in chunks]
        step *= 2

    w2 = [_bdot("nn", l_ka[c], v_s[c]).astype(BF16) for c in chunks]
    p12 = [_bdot("nn", t_inv[c], jnp.concatenate([a_h[c], w2[c]], axis=1)).astype(BF16) for c in chunks]
    qz = [_bdot("nn", g_b[c], p12[c]) for c in chunks]
    q = [(r_h[c].astype(F32) + r_l[c].astype(F32) + qz[c][:, :LANE]).astype(BF16) for c in chunks]
    z_c = [qz[c][:, LANE:] + _bdot("nn", g_k[c], v_s[c]) for c in chunks]
    mn = [_bdot("tn", p12[c], b_h[c]) for c in chunks]
    vk = [_bdot("tn", v_s[c], k_h[c]) for c in chunks]

    s = s_ref[...]
    for c in chunks:
        g_end = jnp.exp(cum_ref[(c + 1) * CHUNK - 1:(c + 1) * CHUNK, :])
        y_st = _bdot("nt", q[c], s) + z_c[c]
        y_ref[c * CHUNK:(c + 1) * CHUNK, :] = y_st[:CHUNK] + y_st[CHUNK:]
        m = (eye + mn[c][:LANE]) * g_end
        n = (mn[c][LANE:] + vk[c]) * g_end
        s = _bdot("nn", s, m) + n
    s_ref[...] = s

    y = y_ref[...]
    inv_n = 1.0 / RWKV_HEAD
    mean = _seg_sum(y, e_ref) * inv_n
    d = y - mean
    var = _seg_sum(d * d, e_ref) * inv_n
    yn = d * lax.rsqrt(var + RWKV_GN_EPS) * lg_ref[...] + lb_ref[...]
    o_ref[...] = ((yn + bonus_ref[...]) * g_ref[...]).astype(o_ref.dtype)


def _rwkv(p, batch, seq, mus, w0, w2p, a0, a2p, g2, k_k, k_a, r_k, lnx_g, lnx_b, tb=512):
    t = p.shape[0]
    w = RWKV_WIDTH
    blocks_per_seq = seq // tb

    def seg(width, base, per_head):
        return pl.BlockSpec((tb, width),
                            lambda b, h, i: (b * blocks_per_seq + i, base // width + (h if per_head else 0)))

    def vec(width, per_head=True):
        return pl.BlockSpec((1, width), lambda b, h, i: (0, h if per_head else 0))

    def mat(rows_):
        return pl.BlockSpec((rows_, LANE), lambda b, h, i: (0, h))

    rows = lax.broadcasted_iota(jnp.int32, (tb, tb), 0)
    cols = lax.broadcasted_iota(jnp.int32, (tb, tb), 1)
    tri = ((rows // CHUNK == cols // CHUNK) & (rows >= cols)).astype(BF16)
    rows = lax.broadcasted_iota(jnp.int32, (LANE, LANE), 0)
    cols = lax.broadcasted_iota(jnp.int32, (LANE, LANE), 1)
    e_seg = (rows // RWKV_HEAD == cols // RWKV_HEAD).astype(BF16)

    in_specs = [seg(LANE, COL_R, True), seg(LANE, COL_K, True), seg(LANE, COL_V, True),
                seg(2 * LANE, COL_WA, False), seg(GATE_LORA, COL_GL, False),
                vec(LANE), vec(LANE), vec(LANE), vec(2 * LANE, False), vec(GATE_LORA, False),
                vec(LANE), mat(LANE), vec(LANE), mat(LANE), mat(GATE_LORA), vec(LANE), vec(LANE), vec(LANE),
                vec(LANE), vec(LANE),
                pl.BlockSpec((tb, tb), lambda b, h, i: (0, 0)),
                pl.BlockSpec((LANE, LANE), lambda b, h, i: (0, 0))]
    half = pltpu.VMEM((tb, LANE), BF16)
    full = pltpu.VMEM((tb, LANE), F32)
    return pl.pallas_call(
        functools.partial(_rwkv_kernel, tb),
        grid=(batch, w // LANE, blocks_per_seq),
        in_specs=in_specs,
        out_specs=pl.BlockSpec((tb, LANE), lambda b, h, i: (b * blocks_per_seq + i, h)),
        out_shape=jax.ShapeDtypeStruct((t, w), BF16),
        scratch_shapes=[pltpu.VMEM((LANE, LANE), F32),
                        pltpu.VMEM((8, LANE), F32), pltpu.VMEM((8, LANE), F32), pltpu.VMEM((8, LANE), F32),
                        pltpu.VMEM((8, 2 * LANE), F32), pltpu.VMEM((8, GATE_LORA), F32)]
                       + [half] * 9 + [full] * 4,
        compiler_params=_params(("parallel", "parallel", "arbitrary")),
        name="rwkv",
    )(p, p, p, p, p, *mus, w0, w2p, a0, a2p, g2, k_k, k_a, r_k, lnx_g, lnx_b, tri, e_seg)


def _gla_kernel(n_chunks, q_ref, k_ref, v_ref, og_ref, al_ref, wal_ref, bal_ref, ng_ref, o_ref, s_ref):
    @pl.when(pl.program_id(2) == 0)
    def _():
        s_ref[...] = jnp.zeros_like(s_ref)

    n_sub = CHUNK // SUB
    scale = GLA_DK ** -0.5
    rows = lax.broadcasted_iota(jnp.int32, (CHUNK, CHUNK), 0)
    cols = lax.broadcasted_iota(jnp.int32, (CHUNK, CHUNK), 1)
    tri = (rows >= cols).astype(BF16)
    srow = lax.broadcasted_iota(jnp.int32, (CHUNK, LANE), 0)
    slane = lax.broadcasted_iota(jnp.int32, (CHUNK, LANE), 1)

    def body(c, carry):
        sl = pl.ds(pl.multiple_of(c * CHUNK, CHUNK), CHUNK)
        q = q_ref[sl, :] * scale
        k = k_ref[sl, :]
        v = v_ref[sl, :]
        pre = _dot3("nn", al_ref[sl, :], wal_ref[...]) + bal_ref[...]
        g = jax.nn.log_sigmoid(pre) / GLA_TAU
        g_hi, g_lo = _split_bf16(g)
        b = _mm(tri, g_hi) + _mm(tri, g_lo)

        pieces = []
        for i in range(n_sub):
            lo, hi = i * SUB, (i + 1) * SUB
            base = b[lo - 1:lo, :] if i > 0 else jnp.zeros((1, LANE), F32)
            q_i = q[lo:hi, :] * jnp.exp(b[lo:hi, :] - base)
            if i > 0:
                k_hat = k[:lo, :] * jnp.exp(base - b[:lo, :])
                top = _bdot("nt", k_hat, q_i)
                pieces.append(jnp.concatenate([top, jnp.zeros((CHUNK - lo, SUB), F32)], axis=0))
            else:
                pieces.append(jnp.zeros((CHUNK, SUB), F32))
        pieces.append(jnp.zeros((CHUNK, LANE - CHUNK), F32))
        a_t = jnp.concatenate(pieces, axis=1)

        for tau in range(SUB):
            q_b = jnp.concatenate([jnp.broadcast_to(q[i * SUB + tau:i * SUB + tau + 1, :], (SUB, LANE))
                                   for i in range(n_sub)], axis=0)
            b_b = jnp.concatenate([jnp.broadcast_to(b[i * SUB + tau:i * SUB + tau + 1, :], (SUB, LANE))
                                   for i in range(n_sub)], axis=0)
            ok = (srow % SUB) <= tau
            dec = jnp.where(ok, jnp.exp(jnp.where(ok, b_b - b, 0.0)), 0.0)
            col = jnp.sum(k * q_b * dec, axis=-1, keepdims=True)
            hit = (slane % SUB == tau) & (slane // SUB == srow // SUB)
            a_t = a_t + jnp.where(hit, col, 0.0)

        s = s_ref[...]
        o = _bdot("tn", a_t[:, :CHUNK], v) + _bdot("nt", q * jnp.exp(b), s)
        b_last = b[CHUNK - 1:CHUNK, :]
        s_ref[...] = s * jnp.exp(b_last) + _bdot("tn", v, k * jnp.exp(b_last - b))

        o = o * lax.rsqrt(jnp.mean(o * o, axis=-1, keepdims=True) + NORM_EPS) * ng_ref[...]
        og = og_ref[sl, :]
        o_ref[sl, :] = (o * (og * jax.nn.sigmoid(og))).astype(o_ref.dtype)
        return carry

    lax.fori_loop(0, n_chunks, body, 0)


def _gla(p, w_alpha2p, b_alpha, norm_g, batch, seq, tb=256):
    t = p.shape[0]
    blocks_per_seq = seq // tb

    def col(width, base):
        return pl.BlockSpec((tb, width), lambda b, h, i: (b * blocks_per_seq + i, base // width + h))

    return pl.pallas_call(
        functools.partial(_gla_kernel, tb // CHUNK),
        grid=(batch, GLA_HEADS, blocks_per_seq),
        in_specs=[col(GLA_DK, COL_GQ), col(GLA_DK, COL_GK), col(GLA_DV, COL_GV), col(GLA_DV, COL_GOG),
                  pl.BlockSpec((tb, LANE), lambda b, h, i: (b * blocks_per_seq + i, COL_GAL // LANE)),
                  pl.BlockSpec((LANE, GLA_DK), lambda b, h, i: (0, h)),
                  pl.BlockSpec((1, GLA_DK), lambda b, h, i: (0, h)),
                  pl.BlockSpec((1, GLA_DV), lambda b, h, i: (0, 0))],
        out_specs=pl.BlockSpec((tb, GLA_DV), lambda b, h, i: (b * blocks_per_seq + i, h)),
        out_shape=jax.ShapeDtypeStruct((t, GLA_VAL_WIDTH), BF16),
        scratch_shapes=[pltpu.VMEM((GLA_DV, GLA_DK), F32)],
        compiler_params=_params(("parallel", "parallel", "arbitrary")),
        name="gla",
    )(p, p, p, p, p, w_alpha2p, b_alpha, norm_g)


def _merge_kernel(ya_ref, yb_ref, pg_ref, bm_ref, wa_ref, wb_ref, o_ref):
    d = wa_ref.shape[1]
    gates = jax.nn.sigmoid(pg_ref[...] + bm_ref[...])
    ma = _mm(ya_ref[...], wa_ref[...])
    mb = _mm(yb_ref[...], wb_ref[...])
    o_ref[...] = (gates[:, :d] * ma + gates[:, d:] * mb).astype(o_ref.dtype)


def _merge(ya, yb, p, b_merge, wa, wb, tm=256):
    t = ya.shape[0]
    d = wa.shape[1]
    return pl.pallas_call(
        _merge_kernel,
        grid=(t // tm,),
        in_specs=[pl.BlockSpec((tm, ya.shape[1]), lambda i: (i, 0)),
                  pl.BlockSpec((tm, yb.shape[1]), lambda i: (i, 0)),
                  pl.BlockSpec((tm, 2 * d), lambda i: (i, COL_GATE // (2 * d))),
                  pl.BlockSpec((1, 2 * d), lambda i: (0, 0)),
                  pl.BlockSpec(wa.shape, lambda i: (0, 0)),
                  pl.BlockSpec(wb.shape, lambda i: (0, 0))],
        out_specs=pl.BlockSpec((tm, d), lambda i: (i, 0)),
        out_shape=jax.ShapeDtypeStruct((t, d), BF16),
        compiler_params=_params(("parallel",)),
        name="merge",
    )(ya, yb, p, b_merge, wa, wb)


def _outproj_kernel(m_ref, x_ref, wo_ref, g_ref, wr_ref, br_ref, x1_ref, h_ref, lg_ref):
    x1 = x_ref[...] + _mm(m_ref[...], wo_ref[...])
    x1_ref[...] = x1
    ms = jnp.mean(x1 * x1, axis=-1, keepdims=True)
    h = x1 * lax.rsqrt(ms + NORM_EPS) * g_ref[...]
    h_ref[...] = h
    lg_ref[...] = _mm(h, wr_ref[...], HI) + br_ref[...]


def _outproj(merged, x2d, w_out, g_ffn, w_router_p, b_router_p, tm=256):
    t, d = x2d.shape
    row = pl.BlockSpec((tm, d), lambda i: (i, 0))
    return pl.pallas_call(
        _outproj_kernel,
        grid=(t // tm,),
        in_specs=[row, row,
                  pl.BlockSpec((d, d), lambda i: (0, 0)),
                  pl.BlockSpec((1, d), lambda i: (0, 0)),
                  pl.BlockSpec((d, LANE), lambda i: (0, 0)),
                  pl.BlockSpec((1, LANE), lambda i: (0, 0))],
        out_specs=[row, row, pl.BlockSpec((tm, LANE), lambda i: (i, 0))],
        out_shape=[jax.ShapeDtypeStruct((t, d), F32), jax.ShapeDtypeStruct((t, d), F32),
                   jax.ShapeDtypeStruct((t, LANE), F32)],
        compiler_params=_params(("parallel",)),
        name="outproj",
    )(merged, x2d, w_out, g_ffn, w_router_p, b_router_p)


def _row_copy(src_ref, src_row, dst_ref, dst_row, sem):
    return pltpu.make_async_copy(src_ref.at[pl.ds(src_row, 1)], dst_ref.at[pl.ds(dst_row, 1)], sem)


def _dispatch_kernel(tm, dest_ref, h_ref, xs_in_ref, xs_ref, sem):
    del xs_in_ref

    def issue(r, carry):
        for j in range(TOP_K):
            _row_copy(h_ref, r, xs_ref, dest_ref[0, 0, r * TOP_K + j], sem).start(priority=j % 2)
        return carry

    lax.fori_loop(0, tm, issue, 0)

    def drain(r, carry):
        for j in range(TOP_K):
            _row_copy(h_ref, r, xs_ref, dest_ref[0, 0, r * TOP_K + j], sem).wait()
        return carry

    lax.fori_loop(0, tm, drain, 0)


def _dispatch(h, dest, n_rows, tm=256):
    t, d = h.shape
    xs0 = jnp.zeros((n_rows, d), h.dtype)
    dest3 = dest.reshape(t // tm, 1, tm * TOP_K)
    return pl.pallas_call(
        functools.partial(_dispatch_kernel, tm),
        grid=(t // tm,),
        in_specs=[pl.BlockSpec((1, 1, tm * TOP_K), lambda i: (i, 0, 0), memory_space=pltpu.SMEM),
                  pl.BlockSpec((tm, d), lambda i: (i, 0)),
                  pl.BlockSpec(memory_space=pl.ANY)],
        out_specs=pl.BlockSpec(memory_space=pl.ANY),
        out_shape=jax.ShapeDtypeStruct((n_rows, d), h.dtype),
        scratch_shapes=[pltpu.SemaphoreType.DMA(())],
        input_output_aliases={2: 0},
        compiler_params=_params(("arbitrary",)),
        name="dispatch",
    )(dest3, h, xs0)


MOE_TILE = 512
MOE_SUB = 256
MOE_SLAB = 512


def _expert_kernel(n_slab, te_ref, ns_ref, nv_ref, x_ref, bg_ref, bu_ref, bd_ref, *rest):
    wg_refs, wu_refs, wd_refs = (rest[q * N_STREAMS:(q + 1) * N_STREAMS] for q in range(3))
    o_ref, xb_ref, gate_ref, up_ref = rest[3 * N_STREAMS:]

    def slab(refs):
        return jnp.concatenate([r[0].astype(BF16) for r in refs], axis=0)

    w = pl.program_id(0)
    s = pl.program_id(1)
    n_sub = ns_ref[w]
    kt = MOE_SLAB

    @pl.when((s == 0) & (n_sub > 0))
    def _():
        for j in range(n_slab):
            xb_ref[j] = x_ref[:, j * kt:(j + 1) * kt].astype(BF16)
            gate_ref[j] = jnp.broadcast_to(bg_ref[0, :, j * kt:(j + 1) * kt], gate_ref.shape[1:])
            up_ref[j] = jnp.broadcast_to(bu_ref[0, :, j * kt:(j + 1) * kt], up_ref.shape[1:])

    @pl.when(s == n_slab)
    def _():
        o_ref[...] = jnp.broadcast_to(bd_ref[0], o_ref.shape)

    @pl.when(s < n_slab)
    def _():
        def sub(i, carry):
            rows = pl.ds(pl.multiple_of(i * MOE_SUB, MOE_SUB), MOE_SUB)
            xb = xb_ref[s, rows, :]
            gate = _mm(xb, slab(wg_refs))
            up = _mm(xb, slab(wu_refs))
            for j in range(n_slab):
                gate_ref[j, rows, :] += gate[:, j * kt:(j + 1) * kt]
                up_ref[j, rows, :] += up[:, j * kt:(j + 1) * kt]
            return carry

        lax.fori_loop(0, n_sub, sub, 0)

    @pl.when(s >= n_slab)
    def _():
        j = s - n_slab

        def sub(i, carry):
            rows = pl.ds(pl.multiple_of(i * MOE_SUB, MOE_SUB), MOE_SUB)
            gate = jnp.minimum(gate_ref[j, rows, :], SWIGLU_LIMIT)
            up = jnp.clip(up_ref[j, rows, :], -SWIGLU_LIMIT, SWIGLU_LIMIT)
            glu = gate * jax.nn.sigmoid(gate * SWIGLU_ALPHA)
            act = ((up + 1.0) * glu).astype(BF16)
            o_ref[rows, :] += _mm(act, slab(wd_refs))
            return carry

        lax.fori_loop(0, n_sub, sub, 0)


def _experts(xs, tile_expert, tile_nsub, n_valid, w_gate, b_gate, w_up, b_up, w_down, b_down):
    n_rows, d = xs.shape
    n_e, _, d_ff = w_gate.shape
    tm, kt = MOE_TILE, MOE_SLAB
    n_tiles = n_rows // tm
    n_slab = d // kt
    assert d == d_ff and d % kt == 0

    def tile(w, nv):
        return jnp.minimum(w, nv[0] - 1)

    def up_slab(w, s, ns):
        return jnp.where(ns[w] > 0, jnp.minimum(s, n_slab - 1), n_slab - 1)

    def down_slab(w, s, ns):
        return jnp.where(ns[w] > 0, jnp.maximum(s - n_slab, 0), n_slab - 1)

    def piece_spec(slab_of, q):
        return pl.BlockSpec((1, kt // N_STREAMS, d),
                            lambda w, s, te, ns, nv: (te[w], slab_of(w, s, ns) * N_STREAMS + q, 0))

    grid_spec = pltpu.PrefetchScalarGridSpec(
        num_scalar_prefetch=3,
        grid=(n_tiles, 2 * n_slab),
        in_specs=[pl.BlockSpec((tm, d), lambda w, s, te, ns, nv: (tile(w, nv), 0)),
                  pl.BlockSpec((1, 1, d_ff), lambda w, s, te, ns, nv: (te[w], 0, 0)),
                  pl.BlockSpec((1, 1, d_ff), lambda w, s, te, ns, nv: (te[w], 0, 0)),
                  pl.BlockSpec((1, 1, d), lambda w, s, te, ns, nv: (te[w], 0, 0))]
                 + [piece_spec(up_slab, q) for q in range(N_STREAMS)] * 2
                 + [piece_spec(down_slab, q) for q in range(N_STREAMS)],
        out_specs=pl.BlockSpec((tm, d), lambda w, s, te, ns, nv: (w, 0)),
        scratch_shapes=[pltpu.VMEM((n_slab, tm, kt), BF16), pltpu.VMEM((n_slab, tm, kt), F32),
                        pltpu.VMEM((n_slab, tm, kt), F32)],
    )
    return pl.pallas_call(
        functools.partial(_expert_kernel, n_slab),
        grid_spec=grid_spec,
        out_shape=jax.ShapeDtypeStruct((n_rows, d), F32),
        compiler_params=_params(("arbitrary", "arbitrary")),
        name="experts",
    )(tile_expert, tile_nsub, n_valid, xs, b_gate.reshape(n_e, 1, d_ff), b_up.reshape(n_e, 1, d_ff),
      b_down.reshape(n_e, 1, d), *([w_gate] * N_STREAMS), *([w_up] * N_STREAMS), *([w_down] * N_STREAMS))


def _expert_cols_kernel(te_ref, ns_ref, nv_ref, x_ref, wg_ref, bg_ref, wu_ref, bu_ref, wd_ref, bd_ref, o_ref, xb_ref):
    w = pl.program_id(0)

    @pl.when(pl.program_id(1) == 0)
    def _():
        xb_ref[...] = x_ref[...].astype(BF16)
        o_ref[...] = jnp.broadcast_to(bd_ref[0], o_ref.shape)

    @pl.when(ns_ref[w] > 0)
    def _():
        xb = xb_ref[...]
        gate = _mm(xb, wg_ref[0].astype(BF16)) + bg_ref[0]
        up = _mm(xb, wu_ref[0].astype(BF16)) + bu_ref[0]
        gate = jnp.minimum(gate, SWIGLU_LIMIT)
        up = jnp.clip(up, -SWIGLU_LIMIT, SWIGLU_LIMIT)
        glu = gate * jax.nn.sigmoid(gate * SWIGLU_ALPHA)
        act = ((up + 1.0) * glu).astype(BF16)
        o_ref[...] += _mm(act, wd_ref[0].astype(BF16))


def _experts_cols(xs, tile_expert, tile_nsub, n_valid, w_gate, b_gate, w_up, b_up, w_down, b_down, tf=512):
    n_rows, d = xs.shape
    n_e, _, d_ff = w_gate.shape
    tm = MOE_TILE
    n_tiles = n_rows // tm
    n_f = d_ff // tf

    def tile(w, nv):
        return jnp.minimum(w, nv[0] - 1)

    def fidx(w, f, ns):
        return jnp.where(ns[w] > 0, f, n_f - 1)

    grid_spec = pltpu.PrefetchScalarGridSpec(
        num_scalar_prefetch=3,
        grid=(n_tiles, n_f),
        in_specs=[pl.BlockSpec((tm, d), lambda w, f, te, ns, nv: (tile(w, nv), 0)),
                  pl.BlockSpec((1, d, tf), lambda w, f, te, ns, nv: (te[w], 0, fidx(w, f, ns))),
                  pl.BlockSpec((1, 1, tf), lambda w, f, te, ns, nv: (te[w], 0, fidx(w, f, ns))),
                  pl.BlockSpec((1, d, tf), lambda w, f, te, ns, nv: (te[w], 0, fidx(w, f, ns))),
                  pl.BlockSpec((1, 1, tf), lambda w, f, te, ns, nv: (te[w], 0, fidx(w, f, ns))),
                  pl.BlockSpec((1, tf, d), lambda w, f, te, ns, nv: (te[w], fidx(w, f, ns), 0)),
                  pl.BlockSpec((1, 1, d), lambda w, f, te, ns, nv: (te[w], 0, 0))],
        out_specs=pl.BlockSpec((tm, d), lambda w, f, te, ns, nv: (w, 0)),
        scratch_shapes=[pltpu.VMEM((tm, d), BF16)],
    )
    return pl.pallas_call(
        _expert_cols_kernel,
        grid_spec=grid_spec,
        out_shape=jax.ShapeDtypeStruct((n_rows, d), F32),
        compiler_params=_params(("arbitrary", "arbitrary")),
        name="experts",
    )(tile_expert, tile_nsub, n_valid, xs, w_gate, b_gate.reshape(n_e, 1, d_ff), w_up,
      b_up.reshape(n_e, 1, d_ff), w_down, b_down.reshape(n_e, 1, d))


def _combine_kernel(tm, final, pos_ref, eo_ref, x1_ref, wt_ref, g_ref, o_ref, buf_ref, sem):
    def issue(r, carry):
        for j in range(TOP_K):
            _row_copy(eo_ref, pos_ref[0, 0, r * TOP_K + j], buf_ref.at[j], r, sem).start(priority=j % 2)
        return carry

    lax.fori_loop(0, tm, issue, 0)

    def drain(r, carry):
        for j in range(TOP_K):
            _row_copy(eo_ref, pos_ref[0, 0, r * TOP_K + j], buf_ref.at[j], r, sem).wait()
        return carry

    lax.fori_loop(0, tm, drain, 0)

    wt = wt_ref[...]
    y = x1_ref[...]
    for j in range(TOP_K):
        y = y + buf_ref[j] * wt[:, j:j + 1]
    if final:
        y = y * lax.rsqrt(jnp.mean(y * y, axis=-1, keepdims=True) + NORM_EPS) * g_ref[...]
    o_ref[...] = y


def _combine(eo, pos, x1, top_w, g_final, final, tm=128):
    t, d = x1.shape
    pos3 = pos.reshape(t // tm, 1, tm * TOP_K)
    return pl.pallas_call(
        functools.partial(_combine_kernel, tm, final),
        grid=(t // tm,),
        in_specs=[pl.BlockSpec((1, 1, tm * TOP_K), lambda i: (i, 0, 0), memory_space=pltpu.SMEM),
                  pl.BlockSpec(memory_space=pl.ANY),
                  pl.BlockSpec((tm, d), lambda i: (i, 0)),
                  pl.BlockSpec((tm, TOP_K), lambda i: (i, 0)),
                  pl.BlockSpec((1, d), lambda i: (0, 0))],
        out_specs=pl.BlockSpec((tm, d), lambda i: (i, 0)),
        out_shape=jax.ShapeDtypeStruct((t, d), F32),
        scratch_shapes=[pltpu.VMEM((TOP_K, tm, d), F32), pltpu.SemaphoreType.DMA(())],
        compiler_params=_params(("arbitrary",)),
        name="combine",
    )(pos3, eo, x1, top_w, g_final)


def _routing(logits):
    t = logits.shape[0]
    tm = MOE_TILE
    top_vals, top_idx = lax.top_k(logits[:, :N_EXPERTS], TOP_K)
    top_w = jax.nn.softmax(top_vals, axis=-1)
    e_flat = top_idx.reshape(-1).astype(jnp.int32)
    onehot = (e_flat[:, None] == jnp.arange(N_EXPERTS, dtype=jnp.int32)[None, :]).astype(jnp.int32)
    incl = jnp.cumsum(onehot, axis=0)
    counts = incl[-1]
    rank = jnp.sum((incl - onehot) * onehot, axis=1)
    tiles = (counts + tm - 1) // tm
    tile_end = jnp.cumsum(tiles)
    tile_start = tile_end - tiles
    dest = jnp.sum(onehot * (tile_start * tm)[None, :], axis=1) + rank
    n_tiles = (t * TOP_K) // tm + N_EXPERTS
    wid = jnp.arange(n_tiles, dtype=jnp.int32)
    n_valid = tile_end[-1:].astype(jnp.int32)
    tile_expert = jnp.minimum(jnp.searchsorted(tile_end, jnp.minimum(wid, n_valid[0] - 1), side="right"),
                              N_EXPERTS - 1).astype(jnp.int32)
    tile_rows = jnp.clip(counts[tile_expert] - (wid - tile_start[tile_expert]) * tm, 0, tm)
    tile_nsub = jnp.where(wid < n_valid[0], (tile_rows + MOE_SUB - 1) // MOE_SUB, 0).astype(jnp.int32)
    return top_w, dest.astype(jnp.int32), tile_expert, tile_nsub, n_valid, n_tiles * tm


def _pad_cols(a, width):
    return jnp.pad(a, ((0, 0), (0, width - a.shape[1])))


def _pad_rows(a, height):
    return jnp.pad(a, ((0, height - a.shape[0]), (0, 0)))


def _split_cols(a, sizes):
    out, acc = [], 0
    for s in sizes:
        out.append(a[..., acc:acc + s])
        acc += s
    return out


def kernel(x, norm_mix_g, w_in, token_shift_mu, rwkv_w0, rwkv_w2, rwkv_a0, rwkv_a2, rwkv_g2, rwkv_k_k, rwkv_k_a, rwkv_r_k, rwkv_lnx_g, rwkv_lnx_b, gla_w_alpha2, gla_b_alpha, gla_norm_g, b_merge, w_branch_rwkv, w_branch_gla, w_out, norm_ffn_g, w_router, b_router, w_gate, b_gate, w_up, b_up, w_down, b_down, norm_final_g):
    batch, seq, d = x.shape
    t = batch * seq
    depth = w_in.shape[0]
    w = RWKV_WIDTH
    sizes = (w, w, w, DECAY_LORA, ICL_LORA, GATE_LORA, GLA_KEY_WIDTH, GLA_KEY_WIDTH, GLA_VAL_WIDTH,
             GLA_VAL_WIDTH, GLA_GATE_LORA, 2 * d)

    x2d = x.reshape(t, d)
    for l in range(depth):
        s_r, s_k, s_v, s_wl, s_al, s_gl, s_gq, s_gk, s_gv, s_gog, s_gal, s_gate = _split_cols(w_in[l], sizes)
        w_p = jnp.concatenate(
            [s_gate, s_r, s_k, s_v, _pad_cols(s_wl, LANE), _pad_cols(s_al, LANE), s_gl, s_gq, s_gk, s_gv,
             s_gog, _pad_cols(s_gal, LANE)], axis=1).astype(BF16)
        mu = token_shift_mu[l][None, :]
        m_r, m_k, m_v, m_wl, m_al, m_gl = _split_cols(mu, sizes[:6])
        mus = (m_r, m_k, m_v, jnp.concatenate([_pad_cols(m_wl, LANE), _pad_cols(m_al, LANE)], axis=1), m_gl)

        p = _inproj(x2d, norm_mix_g[l][None, :], w_p)

        ya = _rwkv(p, batch, seq, mus, rwkv_w0[l][None, :], _pad_rows(rwkv_w2[l], LANE), rwkv_a0[l][None, :],
                   _pad_rows(rwkv_a2[l], LANE), rwkv_g2[l], rwkv_k_k[l][None, :], rwkv_k_a[l][None, :],
                   rwkv_r_k[l].reshape(1, w), rwkv_lnx_g[l][None, :], rwkv_lnx_b[l][None, :])

        yb = _gla(p, _pad_rows(gla_w_alpha2[l], LANE), gla_b_alpha[l][None, :], gla_norm_g[l][None, :],
                  batch, seq)

        merged = _merge(ya, yb, p, b_merge[l][None, :], w_branch_rwkv[l].astype(BF16),
                        w_branch_gla[l].astype(BF16))
        b_router_p = jnp.concatenate([b_router[l], jnp.full((LANE - N_EXPERTS,), -1e30, F32)])[None, :]
        x1, h2, logits = _outproj(merged, x2d, w_out[l].astype(BF16), norm_ffn_g[l][None, :],
                                  _pad_cols(w_router[l], LANE), b_router_p)

        top_w, dest, tile_expert, tile_nsub, n_valid, n_rows = _routing(logits)
        xs = _dispatch(h2, dest, n_rows)
        eo = _experts_cols(xs, tile_expert, tile_nsub, n_valid, w_gate[l], b_gate[l], w_up[l], b_up[l],
                           w_down[l], b_down[l])
        final = l == depth - 1
        x2d = _combine(eo, dest, x1, top_w, norm_final_g[None, :], final)
    return x2d.reshape(batch, seq, d)
```
